```python
import math
import jax
import jax.numpy as jnp
from jax import lax
import numpy as np

D_MODEL = 1024
BATCH = 8
SEQ = 2048
DEPTH = 4
DEC_BATCH = 32
DEC_SEQ = 4
PAST_LEN = 16384
PAGE_SIZE = 128

MLA_HEADS = 8
MLA_Q_LORA = 256
MLA_KV_LORA = 128
MLA_NOPE = 64
MLA_ROPE = 32
MLA_V = 64
ROPE_THETA = 10000.0
MLA_SCALE = (MLA_NOPE + MLA_ROPE) ** -0.5
SB_HEADS = 8
SB_DIM = 64
SB_SCALE = SB_DIM ** -0.5
HG_HEADS = 4
HG_DK = 128
HG_DV = 128
HG_CHUNK = 64
N_BRANCH = 3
BRANCH_W = MLA_HEADS * MLA_V
IN_COLS = MLA_Q_LORA + MLA_KV_LORA + MLA_ROPE + 3 * BRANCH_W + 2 * HG_HEADS * HG_DK + 2 * BRANCH_W + N_BRANCH * D_MODEL
N_EXPERTS = 32
TOP_K = 4
D_FF = 1024
SWIGLU_LIMIT = 7.0
SWIGLU_ALPHA = 1.702
MOE_BLOCK = 128
Q_BLOCK = 128
EPS = 1e-6
F32 = jnp.float32

kernel_name = 'hybrid_mla_stickbreak_hgrn2_moe_step'


def rmsnorm(x, g):
    xf = x.astype(F32)
    y = xf * lax.rsqrt(jnp.mean(xf * xf, axis=-1, keepdims=True) + EPS)
    return (y * g.astype(F32)).astype(x.dtype)


def modulate(x, g, shift, scale):
    return rmsnorm(x, g) * (1.0 + scale[:, None, :]) + shift[:, None, :]


def adaln(c, w, b):
    return jnp.split(jax.nn.silu(c) @ w + b, 6, axis=-1)


def rope(x, pos):
    half = MLA_ROPE // 2
    inv = ROPE_THETA ** (-jnp.arange(half, dtype=F32) / half)
    ang = pos.astype(F32)[:, None] * inv[None, :]
    shape = (1, pos.shape[0]) + (1,) * (x.ndim - 3) + (half,)
    cos = jnp.cos(ang).reshape(shape)
    sin = jnp.sin(ang).reshape(shape)
    xf = x.astype(F32)
    x1, x2 = xf[..., :half], xf[..., half:]
    return jnp.concatenate([x1 * cos - x2 * sin, x2 * cos + x1 * sin], axis=-1).astype(x.dtype)


def to_blocks(a, qb):
    B, T = a.shape[:2]
    return jnp.moveaxis(a.reshape((B, T // qb, qb) + a.shape[2:]), 1, 0)


def from_blocks(a):
    a = jnp.moveaxis(a, 0, 1)
    return a.reshape((a.shape[0], a.shape[1] * a.shape[2]) + a.shape[3:])


def mixer_inputs(h, pos, lb, w_in_l, q_norm_l, kv_norm_l, w_uq_l, w_uk_l):
    B, T, _ = h.shape
    sizes = (MLA_Q_LORA, MLA_KV_LORA, MLA_ROPE, BRANCH_W, BRANCH_W, BRANCH_W,
             HG_HEADS * HG_DK, HG_HEADS * HG_DK, BRANCH_W, BRANCH_W, N_BRANCH * D_MODEL)
    idx = [int(v) for v in np.cumsum(sizes)[:-1]]
    u = h @ w_in_l
    cq, ckv, kr, sq, sk, sv, hf, hq, hi, hgate, gates = jnp.split(u, idx, axis=-1)
    q = (rmsnorm(cq, q_norm_l) @ w_uq_l).reshape(B, T, MLA_HEADS, MLA_NOPE + MLA_ROPE)
    q_pe = rope(q[..., MLA_NOPE:], pos)
    q_lat = jnp.einsum('bthn,chn->bthc', q[..., :MLA_NOPE], w_uk_l)
    ckv = rmsnorm(ckv, kv_norm_l)
    kr = rope(kr, pos)
    sq = sq.reshape(B, T, SB_HEADS, SB_DIM)
    sk = sk.reshape(B, T, SB_HEADS, SB_DIM)
    sv = sv.reshape(B, T, SB_HEADS, SB_DIM)
    f = lb + (1.0 - lb) * jax.nn.sigmoid(hf.astype(F32))
    hg_logf = jnp.log(f).reshape(B, T, HG_HEADS, HG_DK)
    hg_k = (1.0 - f).reshape(B, T, HG_HEADS, HG_DK)
    hg_q = hq.reshape(B, T, HG_HEADS, HG_DK)
    hg_v = hi.reshape(B, T, HG_HEADS, HG_DV)
    gates = gates.reshape(B, T, N_BRANCH, D_MODEL)
    return (q_lat, q_pe, ckv, kr, sq, sk, sv, hg_q, hg_k, hg_v, hg_logf, hgate, gates)


def mla_attend(q_lat, q_pe, ckv, kr, q_pos, k_pos):
    s = (jnp.einsum('bqhc,bkc->bhqk', q_lat, ckv) + jnp.einsum('bqhr,bkr->bhqk', q_pe, kr)).astype(F32) * MLA_SCALE
    s = jnp.where(k_pos[None, :] <= q_pos[:, None], s, -jnp.inf)
    p = jax.nn.softmax(s, axis=-1)
    return jnp.einsum('bhqk,bkc->bqhc', p.astype(ckv.dtype), ckv)


def mla_prompt(q_lat, q_pe, ckv, kr, pos):
    qb = min(Q_BLOCK, q_lat.shape[1])

    def blk(args):
        ql, qp, pi = args
        return mla_attend(ql, qp, ckv, kr, pi, pos)

    o = lax.map(blk, (to_blocks(q_lat, qb), to_blocks(q_pe, qb), pos.reshape(-1, qb)))
    return from_blocks(o)


def sb_block(q, k, v, q_pos, k_pos, log_rest):
    z = jnp.einsum('bqhd,bkhd->bhqk', q, k).astype(F32) * SB_SCALE
    valid = k_pos[None, :] < q_pos[:, None]
    log_keep = jnp.where(valid, jax.nn.log_sigmoid(-z), 0.0)
    log_between = lax.cumsum(log_keep, axis=3, reverse=True) - log_keep
    a = jnp.where(valid, jnp.exp(jax.nn.log_sigmoid(z) + log_between + log_rest[..., None]), 0.0)
    o = jnp.einsum('bhqk,bkhd->bqhd', a, v.astype(F32))
    return o, log_rest + jnp.sum(log_keep, axis=-1)


def sb_prompt(q, k, v, pos):
    B = q.shape[0]
    qb = min(Q_BLOCK, q.shape[1])

    def blk(args):
        qi, pi = args
        o, _ = sb_block(qi, k, v, pi, pos, jnp.zeros((B, SB_HEADS, qb), F32))
        return o

    o = lax.map(blk, (to_blocks(q, qb), pos.reshape(-1, qb)))
    return from_blocks(o).astype(q.dtype)


def sb_sample(q, k_new, v_new, cache_k, cache_v, layer, page_table, q_pos):
    B, T = q.shape[:2]
    o, log_rest = sb_block(q, k_new, v_new, q_pos, q_pos, jnp.zeros((B, SB_HEADS, T), F32))

    def page_step(carry, xs):
        o_acc, rest = carry
        pages, p_idx = xs
        k_p = cache_k[layer, pages].astype(q.dtype)
        v_p = cache_v[layer, pages].astype(q.dtype)
        k_pos = p_idx * PAGE_SIZE + jnp.arange(PAGE_SIZE)
        o_p, rest = sb_block(q, k_p, v_p, q_pos, k_pos, rest)
        return (o_acc + o_p, rest), None

    (o, _), _ = lax.scan(page_step, (o, log_rest), (page_table.T, jnp.arange(page_table.shape[1])), reverse=True)
    return o.astype(q.dtype)


def gla_chunked(q, k, v, log_f, s0):
    B, T, H, _ = q.shape
    C = math.gcd(T, HG_CHUNK)
    NC = T // C

    def chunks(a):
        return a.astype(F32).reshape(B, NC, C, H, -1).transpose(1, 0, 3, 2, 4)

    causal = jnp.arange(C)[:, None] >= jnp.arange(C)[None, :]

    def step(s, xs):
        qc, kc, vc, gc = xs
        bc = jnp.cumsum(gc, axis=2)
        b_end = bc[:, :, -1:, :]
        inter = jnp.einsum('bhtd,bhde->bhte', qc * jnp.exp(bc), s)
        decay = jnp.exp(jnp.where(causal[:, :, None], bc[:, :, :, None, :] - bc[:, :, None, :, :], -jnp.inf))
        att = jnp.einsum('bhtd,bhsd,bhtsd->bhts', qc, kc, decay)
        intra = jnp.einsum('bhts,bhse->bhte', att, vc)
        s_new = jnp.exp(b_end[:, :, 0, :, None]) * s + jnp.einsum('bhsd,bhse->bhde', kc * jnp.exp(b_end - bc), vc)
        return s_new, inter + intra

    s_fin, o = lax.scan(step, s0.astype(F32), (chunks(q), chunks(k), chunks(v), chunks(log_f)))
    return o.transpose(1, 0, 3, 2, 4).reshape(B, T, H, -1), s_fin


def mixer_output(o_lat, o_sb, o_hg, hgate, gates, w_uv_l, hg_norm_l, w_branch_l, w_out_l):
    B, T = o_lat.shape[:2]
    o_mla = jnp.einsum('bthc,chv->bthv', o_lat, w_uv_l).reshape(B, T, BRANCH_W)
    o_hg = rmsnorm(o_hg.astype(hgate.dtype), hg_norm_l).reshape(B, T, BRANCH_W) * jax.nn.silu(hgate)
    o_sb = o_sb.reshape(B, T, BRANCH_W).astype(hgate.dtype)
    br = jnp.stack([o_mla, o_sb, o_hg], axis=2)
    proj = jnp.einsum('btnw,nwd->btnd', br, w_branch_l)
    merged = jnp.sum(jax.nn.sigmoid(gates) * proj, axis=2)
    return merged @ w_out_l


def moe(h, router_w, router_b, w1, b1, w2, b2):
    N, D = h.shape
    logits = (h @ router_w + router_b).astype(F32)
    top_val, top_idx = lax.top_k(logits, TOP_K)
    wts = jax.nn.softmax(top_val, axis=-1).astype(h.dtype)
    M = N * TOP_K
    e_flat = top_idx.reshape(M)
    tok_flat = jnp.repeat(jnp.arange(N), TOP_K)
    w_flat = wts.reshape(M)
    order = jnp.argsort(e_flat * M + jnp.arange(M))
    e_s, tok_s, w_s = e_flat[order], tok_flat[order], w_flat[order]
    counts = jnp.bincount(e_flat, length=N_EXPERTS)
    padded = (counts + MOE_BLOCK - 1) // MOE_BLOCK * MOE_BLOCK
    start = jnp.cumsum(counts) - counts
    pend = jnp.cumsum(padded)
    pstart = pend - padded
    dest = pstart[e_s] + jnp.arange(M) - start[e_s]
    n_blocks = -(-M // MOE_BLOCK) + N_EXPERTS
    x_pad = jnp.zeros((n_blocks * MOE_BLOCK, D), h.dtype).at[dest].set(h[tok_s])
    block_e = jnp.minimum(jnp.searchsorted(pend, jnp.arange(n_blocks) * MOE_BLOCK, side='right'), N_EXPERTS - 1)

    def expert_block(args):
        xb, e = args
        u = xb @ w1[e] + b1[e]
        g = jnp.minimum(u[:, :D_FF], SWIGLU_LIMIT)
        lin = jnp.clip(u[:, D_FF:], -SWIGLU_LIMIT, SWIGLU_LIMIT)
        act = g * jax.nn.sigmoid(SWIGLU_ALPHA * g) * (lin + 1.0)
        return act @ w2[e] + b2[e]

    y_pad = lax.map(expert_block, (x_pad.reshape(n_blocks, MOE_BLOCK, D), block_e)).reshape(-1, D)
    return jnp.zeros((N, D), h.dtype).at[tok_s].add(y_pad[dest] * w_s[:, None])


def moe_ffn(h, router_w, router_b, w1, b1, w2, b2):
    B, T, D = h.shape
    return moe(h.reshape(B * T, D), router_w, router_b, w1, b1, w2, b2).reshape(B, T, D)


def setup_inputs(seed: int = 0) -> dict:
    key = jax.random.key(seed)
    ks = jax.random.split(key, 32)
    n_pages = PAST_LEN // PAGE_SIZE
    pool = (DEC_BATCH * n_pages * 5) // 4

    def nrm(k, shape, s):
        return jax.random.normal(k, shape, F32) * s

    def gain(k, shape):
        return 1.0 + 0.01 * jax.random.normal(k, shape, F32)

    page_table = jax.random.permutation(ks[9], pool)[:DEC_BATCH * n_pages].reshape(DEC_BATCH, n_pages).astype(jnp.int32)
    return {
        'x_prompt': nrm(ks[0], (BATCH, SEQ, D_MODEL), 1.0),
        'x_sample': nrm(ks[1], (DEC_BATCH, DEC_SEQ, D_MODEL), 1.0),
        'c_prompt': nrm(ks[2], (BATCH, D_MODEL), 1.0),
        'c_sample': nrm(ks[3], (DEC_BATCH, D_MODEL), 1.0),
        'cache_mla_ckv': nrm(ks[4], (DEPTH, pool, PAGE_SIZE, MLA_KV_LORA), 1.0),
        'cache_mla_krope': nrm(ks[5], (DEPTH, pool, PAGE_SIZE, MLA_ROPE), 1.0),
        'cache_sb_k': nrm(ks[6], (DEPTH, pool, PAGE_SIZE, SB_HEADS, SB_DIM), 1.0),
        'cache_sb_v': nrm(ks[7], (DEPTH, pool, PAGE_SIZE, SB_HEADS, SB_DIM), 1.0),
        'state_hgrn': nrm(ks[8], (DEPTH, DEC_BATCH, HG_HEADS, HG_DK, HG_DV), 0.5),
        'page_table': page_table,
        'ada_w': nrm(ks[10], (DEPTH, D_MODEL, 6 * D_MODEL), 0.5 * D_MODEL ** -0.5),
        'ada_b': nrm(ks[11], (DEPTH, 6 * D_MODEL), 0.02),
        'norm_mix': gain(ks[12], (DEPTH, D_MODEL)),
        'norm_ffn': gain(ks[13], (DEPTH, D_MODEL)),
        'w_in': nrm(ks[14], (DEPTH, D_MODEL, IN_COLS), D_MODEL ** -0.5),
        'mla_q_norm': gain(ks[15], (DEPTH, MLA_Q_LORA)),
        'mla_kv_norm': gain(ks[16], (DEPTH, MLA_KV_LORA)),
        'mla_w_uq': nrm(ks[17], (DEPTH, MLA_Q_LORA, MLA_HEADS * (MLA_NOPE + MLA_ROPE)), MLA_Q_LORA ** -0.5),
        'mla_w_uk': nrm(ks[18], (DEPTH, MLA_KV_LORA, MLA_HEADS, MLA_NOPE), MLA_KV_LORA ** -0.5),
        'mla_w_uv': nrm(ks[19], (DEPTH, MLA_KV_LORA, MLA_HEADS, MLA_V), MLA_KV_LORA ** -0.5),
        'hg_lower_bounds': nrm(ks[20], (DEPTH, HG_HEADS * HG_DK), 0.1),
        'hg_norm': gain(ks[21], (DEPTH, HG_DV)),
        'w_branch': nrm(ks[22], (DEPTH, N_BRANCH, BRANCH_W, D_MODEL), BRANCH_W ** -0.5),
        'w_out': nrm(ks[23], (DEPTH, D_MODEL, D_MODEL), D_MODEL ** -0.5),
        'router_w': nrm(ks[24], (DEPTH, D_MODEL, N_EXPERTS), D_MODEL ** -0.5),
        'router_b': nrm(ks[25], (DEPTH, N_EXPERTS), 0.01),
        'exp_w1': nrm(ks[26], (DEPTH, N_EXPERTS, D_MODEL, 2 * D_FF), D_MODEL ** -0.5),
        'exp_b1': nrm(ks[27], (DEPTH, N_EXPERTS, 2 * D_FF), 0.01),
        'exp_w2': nrm(ks[28], (DEPTH, N_EXPERTS, D_FF, D_MODEL), D_FF ** -0.5),
        'exp_b2': nrm(ks[29], (DEPTH, N_EXPERTS, D_MODEL), 0.01),
        'final_norm': gain(ks[30], (D_MODEL,)),
    }


def reference(x_prompt, x_sample, c_prompt, c_sample, cache_mla_ckv, cache_mla_krope, cache_sb_k, cache_sb_v,
              state_hgrn, page_table, ada_w, ada_b, norm_mix, norm_ffn, w_in, mla_q_norm, mla_kv_norm,
              mla_w_uq, mla_w_uk, mla_w_uv, hg_lower_bounds, hg_norm, w_branch, w_out, router_w, router_b,
              exp_w1, exp_b1, exp_w2, exp_b2, final_norm):
    Bp, Tp, _ = x_prompt.shape
    Bs, Ts, _ = x_sample.shape
    past_len = page_table.shape[1] * PAGE_SIZE
    pos_p = jnp.arange(Tp)
    pos_s = past_len + jnp.arange(Ts)
    k_pos_s = jnp.arange(past_len + Ts)
    lb_soft = jax.nn.softmax(hg_lower_bounds.astype(F32), axis=0)
    lower_bound = jnp.cumsum(lb_soft, axis=0) - lb_soft[0]
    xp, xs = x_prompt, x_sample
    p_ckv, p_kr, p_sk, p_sv, p_st = [], [], [], [], []
    s_ckv, s_kr, s_sk, s_sv, s_st = [], [], [], [], []
    for l in range(DEPTH):
        sh_mp, sc_mp, g_mp, sh_fp, sc_fp, g_fp = adaln(c_prompt, ada_w[l], ada_b[l])
        sh_ms, sc_ms, g_ms, sh_fs, sc_fs, g_fs = adaln(c_sample, ada_w[l], ada_b[l])
        hp = modulate(xp, norm_mix[l], sh_mp, sc_mp)
        (ql, qpe, ckv, kr, sq, sk, sv, hq, hk, hv, hlf, hgt, gts) = mixer_inputs(
            hp, pos_p, lower_bound[l], w_in[l], mla_q_norm[l], mla_kv_norm[l], mla_w_uq[l], mla_w_uk[l])
        o_lat = mla_prompt(ql, qpe, ckv, kr, pos_p)
        o_sb = sb_prompt(sq, sk, sv, pos_p)
        o_hg, st = gla_chunked(hq, hk, hv, hlf, jnp.zeros((Bp, HG_HEADS, HG_DK, HG_DV), F32))
        xp = xp + g_mp[:, None, :] * mixer_output(o_lat, o_sb, o_hg, hgt, gts, mla_w_uv[l], hg_norm[l], w_branch[l], w_out[l])
        p_ckv.append(ckv)
        p_kr.append(kr)
        p_sk.append(sk)
        p_sv.append(sv)
        p_st.append(st.astype(state_hgrn.dtype))
        hs = modulate(xs, norm_mix[l], sh_ms, sc_ms)
        (ql_s, qpe_s, ckv_s, kr_s, sq_s, sk_s, sv_s, hq_s, hk_s, hv_s, hlf_s, hgt_s, gts_s) = mixer_inputs(
            hs, pos_s, lower_bound[l], w_in[l], mla_q_norm[l], mla_kv_norm[l], mla_w_uq[l], mla_w_uk[l])
        ckv_all = jnp.concatenate(
            [cache_mla_ckv[l, page_table].reshape(Bs, past_len, MLA_KV_LORA).astype(ckv_s.dtype), ckv_s], axis=1)
        kr_all = jnp.concatenate(
            [cache_mla_krope[l, page_table].reshape(Bs, past_len, MLA_ROPE).astype(kr_s.dtype), kr_s], axis=1)
        o_lat_s = mla_attend(ql_s, qpe_s, ckv_all, kr_all, pos_s, k_pos_s)
        o_sb_s = sb_sample(sq_s, sk_s, sv_s, cache_sb_k, cache_sb_v, l, page_table, pos_s)
        o_hg_s, st_s = gla_chunked(hq_s, hk_s, hv_s, hlf_s, state_hgrn[l])
        xs = xs + g_ms[:, None, :] * mixer_output(o_lat_s, o_sb_s, o_hg_s, hgt_s, gts_s, mla_w_uv[l], hg_norm[l], w_branch[l], w_out[l])
        s_ckv.append(ckv_s)
        s_kr.append(kr_s)
        s_sk.append(sk_s)
        s_sv.append(sv_s)
        s_st.append(st_s.astype(state_hgrn.dtype))
        xp = xp + g_fp[:, None, :] * moe_ffn(modulate(xp, norm_ffn[l], sh_fp, sc_fp), router_w[l], router_b[l],
                                             exp_w1[l], exp_b1[l], exp_w2[l], exp_b2[l])
        xs = xs + g_fs[:, None, :] * moe_ffn(modulate(xs, norm_ffn[l], sh_fs, sc_fs), router_w[l], router_b[l],
                                             exp_w1[l], exp_b1[l], exp_w2[l], exp_b2[l])
    y_prompt = rmsnorm(xp, final_norm)
    y_sample = rmsnorm(xs, final_norm)
    return (y_prompt, y_sample,
            jnp.stack(p_ckv), jnp.stack(p_kr), jnp.stack(p_sk), jnp.stack(p_sv), jnp.stack(p_st),
            jnp.stack(s_ckv), jnp.stack(s_kr), jnp.stack(s_sk), jnp.stack(s_sv), jnp.stack(s_st))
```

```python
import functools

import jax
import jax.numpy as jnp
from jax import lax
from jax.experimental import pallas as pl
from jax.experimental.pallas import tpu as pltpu

F32 = jnp.float32
BF16 = jnp.bfloat16
I32 = jnp.int32

D_MODEL = 1024
MLA_HEADS, MLA_Q_LORA, MLA_KV_LORA, MLA_NOPE, MLA_ROPE, MLA_V = 8, 256, 128, 64, 32, 64
MLA_CAT = MLA_KV_LORA + MLA_ROPE
ROPE_THETA = 10000.0
MLA_SCALE = (MLA_NOPE + MLA_ROPE) ** -0.5
SB_HEADS, SB_DIM = 8, 64
SB_SCALE = SB_DIM ** -0.5
HG_HEADS, HG_DK, HG_DV = 4, 128, 128
BRANCH_W = 512
N_BRANCH = 3
N_EXPERTS, TOP_K, D_FF = 32, 4, 1024
SWIGLU_LIMIT, SWIGLU_ALPHA = 7.0, 1.702
PAGE_SIZE = 128
EPS = 1e-6

V7X_VMEM_BYTES = 64 * 1024 * 1024
VMEM_LIMIT = V7X_VMEM_BYTES * 7 // 8
LANES = 128
TOKEN_TILE = 256
MLA_TQ, MLA_TK = 128, 256
SB_T = 256
HG_CHUNK = 16
PAGES_PER_STEP = 8
MOE_CHUNK = 16
MOE_BLOCK = 256
MOE_SLOTS = -(-(N_EXPERTS + (TOKEN_TILE * TOP_K + N_EXPERTS * 7) // MOE_CHUNK + 1) // 8) * 8
MOE_STAGE_ROWS = MOE_SLOTS * MOE_CHUNK
ROUTE_K_SHIFT = 10
ROUTE_K_STRIDE = 1 << ROUTE_K_SHIFT
XW_COLS = D_MODEL + LANES
NEG = -1e30

W_IN_SIZES = (MLA_Q_LORA, MLA_KV_LORA, MLA_ROPE, BRANCH_W, BRANCH_W, BRANCH_W,
              HG_HEADS * HG_DK, HG_HEADS * HG_DK, BRANCH_W, BRANCH_W, N_BRANCH * D_MODEL)


def _cparams(sem):
    return pltpu.CompilerParams(dimension_semantics=sem, vmem_limit_bytes=VMEM_LIMIT)


def _bdot(a, b):
    return jnp.dot(a, b, preferred_element_type=F32)


def _dot_nt(a, b):
    return lax.dot_general(a, b, (((1,), (1,)), ((), ())), preferred_element_type=F32)


def _dot_tn(a, b):
    return lax.dot_general(a, b, (((0,), (0,)), ((), ())), preferred_element_type=F32)


def _rms(x, g):
    return x * lax.rsqrt(jnp.mean(x * x, axis=-1, keepdims=True) + EPS) * g


def _split2(x):
    hi = x.astype(BF16)
    lo = (x - hi.astype(F32)).astype(BF16)
    return hi, lo


def _split3(x):
    hi = x.astype(BF16)
    r = x - hi.astype(F32)
    mid = r.astype(BF16)
    lo = (r - mid.astype(F32)).astype(BF16)
    return hi, mid, lo


def _adaln_kernel(c_ref, w_ref, b_ref, o_ref):
    c = c_ref[...]
    s = (c * jax.nn.sigmoid(c)).astype(BF16)
    o_ref[...] = _bdot(s, w_ref[...].astype(BF16)) + b_ref[...]


def _adaln(c_all, ada_w, ada_b):
    depth, d, six_d = ada_w.shape
    nc = c_all.shape[0]
    return pl.pallas_call(
        _adaln_kernel,
        grid=(depth, six_d // d),
        in_specs=[pl.BlockSpec((nc, d), lambda l, j: (0, 0)),
                  pl.BlockSpec((None, d, d), lambda l, j: (l, 0, j)),
                  pl.BlockSpec((None, 1, d), lambda l, j: (l, 0, j))],
        out_specs=pl.BlockSpec((None, nc, d), lambda l, j: (l, 0, j)),
        out_shape=jax.ShapeDtypeStruct((depth, nc, six_d), F32),
        compiler_params=_cparams(("parallel", "parallel")),
        name="adaln",
    )(c_all, ada_w, ada_b.reshape(depth, 1, six_d))


def _lower_bound_kernel(x_ref, o_ref):
    x = x_ref[...]
    e = jnp.exp(x - jnp.max(x, axis=0, keepdims=True))
    sm = e / jnp.sum(e, axis=0, keepdims=True)
    acc = jnp.zeros_like(sm[0:1])
    for l in range(x.shape[0]):
        acc = acc + sm[l:l + 1]
        o_ref[l:l + 1, :] = acc - sm[0:1]


def _lower_bound(hg_lower_bounds):
    return pl.pallas_call(
        _lower_bound_kernel,
        out_shape=jax.ShapeDtypeStruct(hg_lower_bounds.shape, F32),
        name="lower_bound",
    )(hg_lower_bounds)


def _mixer_in_kernel(x_ref, gn_ref, sh_ref, sc_ref, w_ref, qn_ref, kvn_ref, wuq_ref, wuk_ref,
                     cos_ref, sin_ref, lb_ref,
                     qcat_ref, kcat_ref, ckv_ref, kr_ref, sq_ref, skb_ref, svb_ref, sk_ref, sv_ref,
                     hq_ref, hk_ref, hv_ref, hg_ref, hgate_ref, sig_ref):
    x = x_ref[...]
    h = (_rms(x, gn_ref[...]) * (1.0 + sc_ref[...]) + sh_ref[...]).astype(BF16)

    def seg(start, width):
        return _bdot(h, w_ref[:, start:start + width])

    u0 = seg(0, 512)
    cq = u0[:, :MLA_Q_LORA]
    ckv = _rms(u0[:, 256:384], kvn_ref[...])
    cos8 = cos_ref[...]
    sin8 = sin_ref[...]
    kr = u0[:, 384:416] * cos8[:, :MLA_ROPE] + u0[:, 416:448] * sin8[:, :MLA_ROPE]
    ckv_ref[...] = ckv
    kr_ref[...] = kr
    kcat_ref[:, :MLA_KV_LORA] = ckv.astype(BF16)
    kcat_ref[:, MLA_KV_LORA:] = kr.astype(BF16)

    qq = _bdot(_rms(cq, qn_ref[...]).astype(BF16), wuq_ref[...])
    pe = qq[:, 512:768] * cos8 + qq[:, 768:1024] * sin8
    for hd in range(MLA_HEADS):
        qn = qq[:, hd * MLA_NOPE:(hd + 1) * MLA_NOPE].astype(BF16)
        qcat_ref[hd, :, :MLA_KV_LORA] = _bdot(qn, wuk_ref[hd]).astype(BF16)
        qcat_ref[hd, :, MLA_KV_LORA:] = pe[:, hd * MLA_ROPE:(hd + 1) * MLA_ROPE].astype(BF16)

    sq_ref[...] = seg(512, 512).astype(BF16)
    sk = seg(1024, 512)
    sk_ref[...] = sk
    skb_ref[...] = sk.astype(BF16)
    sv = seg(1536, 512)
    sv_ref[...] = sv
    svb_ref[...] = sv.astype(BF16)
    lb = lb_ref[...]
    f = lb + (1.0 - lb) * jax.nn.sigmoid(seg(2048, 512))
    hg_ref[...] = jnp.log(f)
    hk_ref[...] = 1.0 - f
    hq_ref[...] = seg(2560, 512)
    hv_ref[...] = seg(3072, 512)
    hgate_ref[...] = seg(3584, 512)
    for n in range(N_BRANCH):
        c0 = 4096 + n * D_MODEL
        sig_ref[:, n * D_MODEL:(n + 1) * D_MODEL] = jax.nn.sigmoid(seg(c0, D_MODEL)).astype(BF16)


def _mixer_in(xall, tile0, nb, nt, gn, sh, sc, w_ext, qn, kvn, wuq, wuk, cos8, sin8, lb):
    tm = TOKEN_TILE
    rows = nb * nt * tm
    rmod = sh.shape[1]
    row_spec = lambda w: pl.BlockSpec((tm, w), lambda b, i: (b * nt + i, 0))
    full = lambda a: pl.BlockSpec(a.shape, lambda b, i: (0,) * a.ndim)
    mod_spec = pl.BlockSpec((None, rmod, D_MODEL), lambda b, i: (b, 0, 0))
    pos_spec = pl.BlockSpec((tm, 256), lambda b, i: (i, 0))
    out_shapes = [
        jax.ShapeDtypeStruct((MLA_HEADS, rows, MLA_CAT), BF16),
        jax.ShapeDtypeStruct((rows, MLA_CAT), BF16),
        jax.ShapeDtypeStruct((rows, MLA_KV_LORA), F32),
        jax.ShapeDtypeStruct((rows, MLA_ROPE), F32),
        jax.ShapeDtypeStruct((rows, 512), BF16),
        jax.ShapeDtypeStruct((rows, 512), BF16),
        jax.ShapeDtypeStruct((rows, 512), BF16),
        jax.ShapeDtypeStruct((rows, 512), F32),
        jax.ShapeDtypeStruct((rows, 512), F32),
        jax.ShapeDtypeStruct((rows, 512), F32),
        jax.ShapeDtypeStruct((rows, 512), F32),
        jax.ShapeDtypeStruct((rows, 512), F32),
        jax.ShapeDtypeStruct((rows, 512), F32),
        jax.ShapeDtypeStruct((rows, 512), F32),
        jax.ShapeDtypeStruct((rows, N_BRANCH * D_MODEL), BF16),
    ]
    out_specs = [pl.BlockSpec((MLA_HEADS, tm, MLA_CAT), lambda b, i: (0, b * nt + i, 0))]
    out_specs += [row_spec(s.shape[1]) for s in out_shapes[1:]]
    return pl.pallas_call(
        _mixer_in_kernel,
        grid=(nb, nt),
        in_specs=[pl.BlockSpec((tm, D_MODEL), lambda b, i: (tile0 + b * nt + i, 0)),
                  full(gn), mod_spec, mod_spec, full(w_ext), full(qn), full(kvn), full(wuq), full(wuk),
                  pos_spec, pos_spec, full(lb)],
        out_specs=out_specs,
        out_shape=out_shapes,
        compiler_params=_cparams(("parallel", "parallel")),
        name="mixer_in",
    )(xall, gn, sh, sc, w_ext, qn, kvn, wuq, wuk, cos8, sin8, lb)


def _mla_prompt_kernel(q_ref, k_ref, wuv_ref, o_ref, *, tq, tk):
    i = pl.program_id(1)
    rows = MLA_HEADS * tq
    q = q_ref[...].reshape(rows, MLA_CAT)
    last = (i * tq + tq - 1) // tk

    def step(j, carry, masked):
        m, l, acc = carry
        k = k_ref[pl.ds(pl.multiple_of(j * tk, tk), tk), :]
        s = _dot_nt(q, k) * MLA_SCALE
        if masked:
            qpos = i * tq + (lax.broadcasted_iota(I32, (rows, tk), 0) & (tq - 1))
            kpos = j * tk + lax.broadcasted_iota(I32, (rows, tk), 1)
            s = jnp.where(kpos <= qpos, s, NEG)
        m_new = jnp.maximum(m, jnp.max(s, axis=1, keepdims=True))
        alpha = jnp.exp(m - m_new)
        p = jnp.exp(s - m_new)
        l = alpha * l + jnp.sum(p, axis=1, keepdims=True)
        acc = alpha * acc + _bdot(p.astype(BF16), k[:, :MLA_KV_LORA])
        return m_new, l, acc

    init = (jnp.full((rows, 1), NEG, F32), jnp.zeros((rows, 1), F32), jnp.zeros((rows, MLA_KV_LORA), F32))
    carry = lax.fori_loop(0, last, lambda j, c: step(j, c, False), init)
    _, l, acc = step(last, carry, True)
    o = (acc / l).astype(BF16)
    for hp in range(MLA_HEADS // 2):
        pair = jnp.concatenate([o[(2 * hp) * tq:(2 * hp + 1) * tq], o[(2 * hp + 1) * tq:(2 * hp + 2) * tq]], axis=1)
        o_ref[:, hp * LANES:(hp + 1) * LANES] = _bdot(pair, wuv_ref[hp]).astype(BF16)


def _mla_prompt(qcat, kcat, wuv_bd, nb, t):
    tq, tk = MLA_TQ, MLA_TK
    nq = t // tq
    return pl.pallas_call(
        functools.partial(_mla_prompt_kernel, tq=tq, tk=tk),
        grid=(nb, nq),
        in_specs=[pl.BlockSpec((MLA_HEADS, tq, MLA_CAT), lambda b, i: (0, b * nq + i, 0)),
                  pl.BlockSpec((t, MLA_CAT), lambda b, i: (b, 0)),
                  pl.BlockSpec(wuv_bd.shape, lambda b, i: (0, 0, 0))],
        out_specs=pl.BlockSpec((tq, BRANCH_W), lambda b, i: (b * nq + i, 0)),
        out_shape=jax.ShapeDtypeStruct((nb * t, BRANCH_W), BF16),
        compiler_params=_cparams(("parallel", "parallel")),
        name="mla_prompt",
    )(qcat, kcat, wuv_bd)


def _neg_softplus(z):
    return -(jnp.maximum(z, 0.0) + jnp.log(1.0 + jnp.exp(-jnp.abs(z))))


def _sb_block(q, k, v, u, rest, valid):
    z = _dot_nt(q, k) * SB_SCALE
    lk = _neg_softplus(z)
    if valid is not None:
        lk = jnp.where(valid, lk, 0.0)
    hi, lo = _split2(lk)
    incl = _bdot(hi, u) + _bdot(lo, u)
    a = jnp.exp(z + incl + rest)
    if valid is not None:
        a = jnp.where(valid, a, 0.0)
    return _bdot(a.astype(BF16), v), rest + incl[:, 0:1]


def _sb_prompt_kernel(q_ref, k_ref, v_ref, u_ref, o_ref, *, t):
    i = pl.program_id(2)
    u = u_ref[...]
    q = q_ref[...]
    row = lax.broadcasted_iota(I32, (t, t), 0)
    col = lax.broadcasted_iota(I32, (t, t), 1)
    valid = col < row
    heads = [slice(hh * SB_DIM, (hh + 1) * SB_DIM) for hh in range(2)]

    def block(j, carry, mask):
        r0 = pl.multiple_of(j * t, t)
        k = k_ref[pl.ds(r0, t), :]
        v = v_ref[pl.ds(r0, t), :]
        out = []
        for hh in range(2):
            acc, rest = carry[hh]
            o, rest = _sb_block(q[:, heads[hh]], k[:, heads[hh]], v[:, heads[hh]], u, rest, mask)
            out.append((acc + o, rest))
        return tuple(out)

    zero = (jnp.zeros((t, SB_DIM), F32), jnp.zeros((t, 1), F32))
    carry = block(i, (zero, zero), valid)
    carry = lax.fori_loop(0, i, lambda jj, c: block(i - 1 - jj, c, None), carry)
    o_ref[...] = jnp.concatenate([carry[0][0], carry[1][0]], axis=1).astype(BF16)


def _sb_prompt(sq, sk, sv, u, nb, t):
    tb = SB_T
    nq = t // tb
    npair = SB_HEADS // 2
    return pl.pallas_call(
        functools.partial(_sb_prompt_kernel, t=tb),
        grid=(nb, npair, nq),
        in_specs=[pl.BlockSpec((tb, LANES), lambda b, hp, i: (b * nq + i, hp)),
                  pl.BlockSpec((t, LANES), lambda b, hp, i: (b, hp)),
                  pl.BlockSpec((t, LANES), lambda b, hp, i: (b, hp)),
                  pl.BlockSpec((tb, tb), lambda b, hp, i: (0, 0))],
        out_specs=pl.BlockSpec((tb, LANES), lambda b, hp, i: (b * nq + i, hp)),
        out_shape=jax.ShapeDtypeStruct((nb * t, BRANCH_W), BF16),
        compiler_params=_cparams(("parallel", "parallel", "parallel")),
        name="sb_prompt",
    )(sq, sk, sv, u)


def _hgrn_kernel(q_ref, k_ref, v_ref, g_ref, s0_ref, l_ref, o_ref, sf_ref, st_ref, bc_ref, *, tt, nt):
    i = pl.program_id(1)
    c = HG_CHUNK

    @pl.when(i == 0)
    def _():
        for hd in range(HG_HEADS):
            st_ref[hd] = s0_ref[hd].T

    lmat = l_ref[...]
    g1, g2, g3 = _split3(g_ref[...])
    bc_ref[...] = _bdot(lmat, g1) + _bdot(lmat, g2) + _bdot(lmat, g3)
    row = lax.broadcasted_iota(I32, (c, HG_DK), 0)

    def chunk(ci, _):
        r0 = pl.multiple_of(ci * c, c)
        for hd in range(HG_HEADS):
            sl = (pl.ds(r0, c), slice(hd * HG_DK, (hd + 1) * HG_DK))
            bc = bc_ref[sl]
            q = q_ref[sl]
            k = k_ref[sl]
            v = v_ref[sl]
            st = st_ref[hd]
            inter = _dot_nt((q * jnp.exp(bc)).astype(BF16), st.astype(BF16))
            intra = jnp.zeros((c, HG_DV), F32)
            for s in range(c):
                decay = jnp.where(row >= s, jnp.exp(bc - bc[s:s + 1, :]), 0.0)
                att = jnp.sum(q * (k[s:s + 1, :] * decay), axis=1, keepdims=True)
                intra = intra + att * v[s:s + 1, :]
            o_ref[sl] = inter + intra
            bend = bc[c - 1:c, :]
            kend = k * jnp.exp(bend - bc)
            st_ref[hd] = st * jnp.exp(bend) + _dot_tn(v.astype(BF16), kend.astype(BF16))
        return 0

    lax.fori_loop(0, tt // c, chunk, 0)

    @pl.when(i == nt - 1)
    def _():
        for hd in range(HG_HEADS):
            sf_ref[hd] = st_ref[hd].T


def _hgrn(hq, hk, hv, hg, s0, nb, t):
    tt = min(t, TOKEN_TILE)
    nt = t // tt
    r = jnp.arange(tt)
    lmat = ((r[:, None] >= r[None, :]) & (r[:, None] // HG_CHUNK == r[None, :] // HG_CHUNK)).astype(BF16)
    row_spec = pl.BlockSpec((tt, 512), lambda b, i: (b * nt + i, 0))
    st_spec = pl.BlockSpec((None, HG_HEADS, HG_DK, HG_DV), lambda b, i: (b, 0, 0, 0))
    return pl.pallas_call(
        functools.partial(_hgrn_kernel, tt=tt, nt=nt),
        grid=(nb, nt),
        in_specs=[row_spec, row_spec, row_spec, row_spec, st_spec,
                  pl.BlockSpec((tt, tt), lambda b, i: (0, 0))],
        out_specs=[row_spec, st_spec],
        out_shape=[jax.ShapeDtypeStruct((nb * t, 512), F32),
                   jax.ShapeDtypeStruct((nb, HG_HEADS, HG_DK, HG_DV), F32)],
        scratch_shapes=[pltpu.VMEM((HG_HEADS, HG_DV, HG_DK), F32), pltpu.VMEM((tt, 512), F32)],
        compiler_params=_cparams(("parallel", "arbitrary")),
        name="hgrn",
    )(hq, hk, hv, hg, s0, lmat)


def _diag_heads(res, mask):
    tdim = res.shape[0] // 8
    return jnp.sum(res.reshape(tdim, 8, BRANCH_W) * mask[None], axis=1)


def _page_pipeline(pt_ref, caches, bufs, sem_ref, *, layer, npg, npp, nsteps, newest_first):
    b = pl.program_id(0)
    j = pl.program_id(1)
    step = b * nsteps + j
    slot = step % 2

    def copies(bb, jj, sl):
        out = []
        for n in range(npp):
            logical = (npg - (jj + 1) * npp + n) if newest_first else (jj * npp + n)
            page = pt_ref[bb * npg + logical]
            for ci, (cache, buf) in enumerate(zip(caches, bufs)):
                out.append(pltpu.make_async_copy(cache.at[layer, page], buf.at[sl, n], sem_ref.at[ci, sl, n]))
        return out

    @pl.when(step == 0)
    def _():
        for cp in copies(b, j, slot):
            cp.start()

    @pl.when(step + 1 < pl.num_programs(0) * nsteps)
    def _():
        wrap = j + 1 == nsteps
        for cp in copies(jnp.where(wrap, b + 1, b), jnp.where(wrap, 0, j + 1), 1 - slot):
            cp.start()

    for cp in copies(b, j, slot):
        cp.wait()
    return slot


def _mla_sample_kernel(pt_ref, q_ref, kn_ref, ckv_hbm, kr_hbm, wuv_ref, mask_ref, o_ref,
                       ckv_buf, kr_buf, sem_ref, m_ref, l_ref, acc_ref, *, layer, npg, npp, nsteps, ts):
    j = pl.program_id(1)
    rows = ts * MLA_HEADS
    q = q_ref[...]
    ql = q[:, :MLA_KV_LORA]
    qp = q[:, MLA_KV_LORA:]
    slot = _page_pipeline(pt_ref, (ckv_hbm, kr_hbm), (ckv_buf, kr_buf), sem_ref, layer=layer, npg=npg,
                          npp=npp, nsteps=nsteps, newest_first=False)

    @pl.when(j == 0)
    def _():
        kn = kn_ref[...]
        nk = kn.shape[0]
        s = _dot_nt(q, kn) * MLA_SCALE
        trow = lax.broadcasted_iota(I32, (rows, nk), 0) // MLA_HEADS
        tcol = lax.broadcasted_iota(I32, (rows, nk), 1)
        s = jnp.where(tcol <= trow, s, NEG)
        m = jnp.max(s, axis=1, keepdims=True)
        p = jnp.exp(s - m)
        m_ref[...] = m
        l_ref[...] = jnp.sum(p, axis=1, keepdims=True)
        acc_ref[...] = _bdot(p.astype(BF16), kn[:, :MLA_KV_LORA])

    cks = [ckv_buf[slot, n].astype(BF16) for n in range(npp)]
    s = jnp.concatenate(
        [_dot_nt(ql, cks[n]) + _dot_nt(qp, kr_buf[slot, n].astype(BF16)) for n in range(npp)], axis=1) * MLA_SCALE
    m_old = m_ref[...]
    m_new = jnp.maximum(m_old, jnp.max(s, axis=1, keepdims=True))
    alpha = jnp.exp(m_old - m_new)
    p = jnp.exp(s - m_new)
    m_ref[...] = m_new
    l_ref[...] = alpha * l_ref[...] + jnp.sum(p, axis=1, keepdims=True)
    acc = alpha * acc_ref[...]
    for n in range(npp):
        acc = acc + _bdot(p[:, n * PAGE_SIZE:(n + 1) * PAGE_SIZE].astype(BF16), cks[n])
    acc_ref[...] = acc

    @pl.when(j == nsteps - 1)
    def _():
        o = (acc_ref[...] / l_ref[...]).astype(BF16)
        o_ref[...] = _diag_heads(_bdot(o, wuv_ref[...]), mask_ref[...]).astype(BF16)


def _mla_sample(layer, page_table, q_s, kn_s, cache_ckv, cache_kr, wuv_all, head_mask):
    bs, npg = page_table.shape
    npp = PAGES_PER_STEP
    nsteps = npg // npp
    rows = q_s.shape[1]
    ts = rows // MLA_HEADS

    in_specs = [pl.BlockSpec((None, rows, MLA_CAT), lambda b, j, pt: (b, 0, 0)),
                pl.BlockSpec((None,) + kn_s.shape[1:], lambda b, j, pt: (b, 0, 0)),
                pl.BlockSpec(memory_space=pl.ANY), pl.BlockSpec(memory_space=pl.ANY),
                pl.BlockSpec(wuv_all.shape, lambda b, j, pt: (0, 0)),
                pl.BlockSpec(head_mask.shape, lambda b, j, pt: (0, 0))]
    return pl.pallas_call(
        functools.partial(_mla_sample_kernel, layer=layer, npg=npg, npp=npp, nsteps=nsteps, ts=ts),
        grid_spec=pltpu.PrefetchScalarGridSpec(
            num_scalar_prefetch=1, grid=(bs, nsteps), in_specs=in_specs,
            out_specs=pl.BlockSpec((None, ts, BRANCH_W), lambda b, j, pt: (b, 0, 0)),
            scratch_shapes=[pltpu.VMEM((2, npp, PAGE_SIZE, MLA_KV_LORA), F32),
                            pltpu.VMEM((2, npp, PAGE_SIZE, MLA_ROPE), F32),
                            pltpu.SemaphoreType.DMA((2, 2, npp)),
                            pltpu.VMEM((rows, 1), F32), pltpu.VMEM((rows, 1), F32),
                            pltpu.VMEM((rows, MLA_KV_LORA), F32)]),
        out_shape=jax.ShapeDtypeStruct((bs, ts, BRANCH_W), BF16),
        compiler_params=_cparams(("arbitrary", "arbitrary")),
        name="mla_sample",
    )(page_table.reshape(-1), q_s, kn_s, cache_ckv, cache_kr, wuv_all, head_mask)


def _sb_sample_kernel(pt_ref, q_ref, kn_ref, vn_ref, k_hbm, v_hbm, u_ref, mask_ref, o_ref,
                      k_buf, v_buf, sem_ref, rest_ref, acc_ref, *, layer, npg, npp, nsteps, ts):
    j = pl.program_id(1)
    rows = ts * SB_HEADS
    q = q_ref[...]
    u = u_ref[...]
    slot = _page_pipeline(pt_ref, (k_hbm, v_hbm), (k_buf, v_buf), sem_ref, layer=layer, npg=npg,
                          npp=npp, nsteps=nsteps, newest_first=True)

    @pl.when(j == 0)
    def _():
        nk = kn_ref.shape[0]
        trow = lax.broadcasted_iota(I32, (rows, nk), 0) // SB_HEADS
        tcol = lax.broadcasted_iota(I32, (rows, nk), 1)
        o, r = _sb_block(q, kn_ref[...], vn_ref[...], u, jnp.zeros((rows, 1), F32), tcol < trow)
        acc_ref[...] = o
        rest_ref[...] = r

    acc = acc_ref[...]
    r = rest_ref[...]
    for n in reversed(range(npp)):
        o, r = _sb_block(q, k_buf[slot, n].astype(BF16), v_buf[slot, n].astype(BF16), u, r, None)
        acc = acc + o
    acc_ref[...] = acc
    rest_ref[...] = r

    @pl.when(j == nsteps - 1)
    def _():
        o_ref[...] = _diag_heads(acc_ref[...], mask_ref[...]).astype(BF16)


def _sb_sample(layer, page_table, q_bd, kn, vn, cache_k, cache_v, u, head_mask):
    bs, npg = page_table.shape
    npp = PAGES_PER_STEP
    nsteps = npg // npp
    rows = q_bd.shape[1]
    ts = rows // SB_HEADS

    new_spec = pl.BlockSpec((None, PAGE_SIZE, BRANCH_W), lambda b, j, pt: (b, 0, 0))
    in_specs = [pl.BlockSpec((None, rows, BRANCH_W), lambda b, j, pt: (b, 0, 0)), new_spec, new_spec,
                pl.BlockSpec(memory_space=pl.ANY), pl.BlockSpec(memory_space=pl.ANY),
                pl.BlockSpec(u.shape, lambda b, j, pt: (0, 0)),
                pl.BlockSpec(head_mask.shape, lambda b, j, pt: (0, 0))]
    return pl.pallas_call(
        functools.partial(_sb_sample_kernel, layer=layer, npg=npg, npp=npp, nsteps=nsteps, ts=ts),
        grid_spec=pltpu.PrefetchScalarGridSpec(
            num_scalar_prefetch=1, grid=(bs, nsteps), in_specs=in_specs,
            out_specs=pl.BlockSpec((None, ts, BRANCH_W), lambda b, j, pt: (b, 0, 0)),
            scratch_shapes=[pltpu.VMEM((2, npp, PAGE_SIZE, BRANCH_W), F32),
                            pltpu.VMEM((2, npp, PAGE_SIZE, BRANCH_W), F32),
                            pltpu.SemaphoreType.DMA((2, 2, npp)),
                            pltpu.VMEM((rows, 1), F32), pltpu.VMEM((rows, BRANCH_W), F32)]),
        out_shape=jax.ShapeDtypeStruct((bs, ts, BRANCH_W), BF16),
        compiler_params=_cparams(("arbitrary", "arbitrary")),
        name="sb_sample",
    )(page_table.reshape(-1), q_bd, kn, vn, cache_k, cache_v, u, head_mask)


def _mixer_out_kernel(*refs, n_compute, has_tail):
    n_in = 15
    outs = refs[n_in + (3 if has_tail else 0):]
    if not has_tail:
        _mixer_out_tile(*refs[:n_in], *outs)
        return
    i = pl.program_id(0)

    @pl.when(i < n_compute)
    def _():
        _mixer_out_tile(*refs[:n_in], *outs)

    @pl.when(i == n_compute)
    def _():
        for src, dst in zip(refs[n_in:n_in + 3], outs):
            dst[...] = src[...]


def _mixer_out_tile(x_ref, omla_ref, osb_ref, ohg_ref, hgate_ref, sig_ref, gm_ref, shf_ref, scf_ref,
                    hgn_ref, wbr_ref, wout_ref, nf_ref, rw_ref, rb_ref,
                    xmid_ref, h2_ref, logit_ref):
    ohg = ohg_ref[...]
    hgn = hgn_ref[...]
    normed = jnp.concatenate(
        [_rms(ohg[:, hd * HG_DV:(hd + 1) * HG_DV], hgn) for hd in range(HG_HEADS)], axis=1)
    gate = hgate_ref[...]
    br_hg = (normed * (gate * jax.nn.sigmoid(gate))).astype(BF16)
    branches = (omla_ref[...], osb_ref[...], br_hg)
    merged = jnp.zeros((x_ref.shape[0], D_MODEL), F32)
    for n in range(N_BRANCH):
        sig = sig_ref[:, n * D_MODEL:(n + 1) * D_MODEL].astype(F32)
        merged = merged + sig * _bdot(branches[n], wbr_ref[n])
    x = x_ref[...] + gm_ref[...] * _bdot(merged.astype(BF16), wout_ref[...])
    xmid_ref[...] = x
    h2 = (_rms(x, nf_ref[...]) * (1.0 + scf_ref[...]) + shf_ref[...]).astype(BF16)
    h2_ref[...] = h2
    logit_ref[...] = _bdot(h2, rw_ref[...]) + rb_ref[...]


def _mixer_out(xall, tile0, nb, nt, omla, osb, ohg, hgate, sig, gm, shf, scf, hgn, wbr, wout, nf, rw, rb, tail):
    tm = TOKEN_TILE
    rmod = gm.shape[1]
    n_compute = nb * nt
    ntile = n_compute + (0 if tail is None else 1)
    last = n_compute - 1
    glob = lambda w: pl.BlockSpec((tm, w), lambda i: (tile0 + jnp.minimum(i, last), 0))
    loc = lambda w: pl.BlockSpec((tm, w), lambda i: (jnp.minimum(i, last), 0))
    out = lambda w: pl.BlockSpec((tm, w), lambda i: (i, 0))
    full = lambda a: pl.BlockSpec(a.shape, lambda i: (0,) * a.ndim)
    mod_spec = pl.BlockSpec((None, rmod, D_MODEL), lambda i: (jnp.minimum(i, last) // nt, 0, 0))
    in_specs = [glob(D_MODEL), loc(512), loc(512), loc(512), loc(512), loc(N_BRANCH * D_MODEL),
                mod_spec, mod_spec, mod_spec, full(hgn), full(wbr), full(wout), full(nf), full(rw), full(rb)]
    args = [xall, omla, osb, ohg, hgate, sig, gm, shf, scf, hgn, wbr, wout, nf, rw, rb]
    if tail is not None:
        in_specs += [full(t) for t in tail]
        args += list(tail)
    return pl.pallas_call(
        functools.partial(_mixer_out_kernel, n_compute=n_compute, has_tail=tail is not None),
        grid=(ntile,),
        in_specs=in_specs,
        out_specs=[out(D_MODEL), out(D_MODEL), out(LANES)],
        out_shape=[jax.ShapeDtypeStruct((ntile * tm, D_MODEL), F32),
                   jax.ShapeDtypeStruct((ntile * tm, D_MODEL), BF16),
                   jax.ShapeDtypeStruct((ntile * tm, LANES), F32)],
        compiler_params=_cparams(("parallel",)),
        name="mixer_out",
    )(*args)


def _route_kernel(logit_ref, lt_ref, ws_ref, lr_ref, lrt_ref, base_ref, tot_ref, carry_ref, *, n_valid, ntiles):
    i = pl.program_id(0)
    tm = TOKEN_TILE

    @pl.when(i == 0)
    def _():
        carry_ref[...] = jnp.zeros_like(carry_ref)

    x = logit_ref[...]
    lane = lax.broadcasted_iota(I32, (tm, LANES), 1)
    rowg = i * tm + lax.broadcasted_iota(I32, (tm, LANES), 0)
    sels, vals = [], []
    for _ in range(TOP_K):
        m = jnp.max(x, axis=1, keepdims=True)
        idx = jnp.min(jnp.where(x == m, lane, LANES), axis=1, keepdims=True)
        sel = lane == idx
        sels.append(sel)
        vals.append(m)
        x = jnp.where(sel, -jnp.inf, x)
    ex = [jnp.exp(v - vals[0]) for v in vals]
    den = ex[0] + ex[1] + ex[2] + ex[3]
    w = jnp.zeros((tm, LANES), F32)
    msk = jnp.zeros((tm, LANES), F32)
    kth = jnp.full((tm, LANES), -1.0, F32)
    live = rowg < n_valid
    for k, (sel, e) in enumerate(zip(sels, ex)):
        pick = sel & live
        w = jnp.where(pick, e / den, w)
        msk = jnp.where(pick, 1.0, msk)
        kth = jnp.where(pick, float(k), kth)
    hi, mid, lo = _split3(w)
    ws = hi.astype(F32) + pltpu.roll(mid.astype(F32), N_EXPERTS, 1) + pltpu.roll(lo.astype(F32), 2 * N_EXPERTS, 1)
    ws_ref[...] = ws.astype(BF16)
    rank = _bdot(lt_ref[...], msk.astype(BF16))
    code = jnp.where(msk > 0.0, rank + ROUTE_K_STRIDE * kth, -1.0)
    lr_ref[...] = code.astype(I32)
    lrt_ref[...] = code.T.astype(I32)
    base_ref[...] = carry_ref[...].astype(I32)
    carry_ref[...] = carry_ref[...] + jnp.sum(msk, axis=0, keepdims=True)

    @pl.when(i == ntiles - 1)
    def _():
        tot_ref[...] = carry_ref[...].astype(I32)


def _route(logits, n_valid):
    npad = logits.shape[0]
    tm = TOKEN_TILE
    ntiles = npad // tm
    r = jnp.arange(tm)
    lt = (r[:, None] > r[None, :]).astype(BF16)
    return pl.pallas_call(
        functools.partial(_route_kernel, n_valid=n_valid, ntiles=ntiles),
        grid=(ntiles,),
        in_specs=[pl.BlockSpec((tm, LANES), lambda i: (i, 0)),
                  pl.BlockSpec((tm, tm), lambda i: (0, 0))],
        out_specs=[pl.BlockSpec((tm, LANES), lambda i: (i, 0)),
                   pl.BlockSpec((tm, LANES), lambda i: (i, 0)),
                   pl.BlockSpec((LANES, tm), lambda i: (0, i)),
                   pl.BlockSpec((None, 1, LANES), lambda i: (i, 0, 0)),
                   pl.BlockSpec((1, LANES), lambda i: (0, 0))],
        out_shape=[jax.ShapeDtypeStruct((npad, LANES), BF16),
                   jax.ShapeDtypeStruct((npad, LANES), I32),
                   jax.ShapeDtypeStruct((LANES, npad), I32),
                   jax.ShapeDtypeStruct((ntiles, 1, LANES), I32),
                   jax.ShapeDtypeStruct((1, LANES), I32)],
        scratch_shapes=[pltpu.VMEM((1, LANES), F32)],
        compiler_params=_cparams(("arbitrary",)),
        name="route",
    )(logits, lt)


def _stage_dest(code, vec, axis):
    rank = code & (ROUTE_K_STRIDE - 1)
    kth = code >> ROUTE_K_SHIFT
    out = []
    for k in range(TOP_K):
        hit = kth == k
        dest = jnp.sum(jnp.where(hit, rank + vec, 0), axis=axis, keepdims=True)
        has = jnp.sum(jnp.where(hit, 1, 0), axis=axis, keepdims=True)
        out.append(jnp.where(has > 0, dest, -1))
    return out


def _scatter_kernel(nslot_ref, dst_ref, srow_ref, a_ref, cnt_ref, h2_ref, ws_ref, codet_ref, vect_ref, xw_in_ref,
                    xw_ref, stage_ref, carry_ref, sem_ref):
    del xw_in_ref
    i = pl.program_id(0)
    tm = h2_ref.shape[0]

    @pl.when(i == 0)
    def _():
        carry_ref[...] = jnp.zeros_like(carry_ref)

    rows = lax.broadcasted_iota(I32, (MOE_STAGE_ROWS, tm), 0)
    p = jnp.zeros((MOE_STAGE_ROWS, tm), F32)
    for dest in _stage_dest(codet_ref[...], vect_ref[...], 0):
        p = jnp.where(rows == dest, 1.0, p)
    p = p.astype(BF16)
    stage_ref[:, :D_MODEL] = _bdot(p, h2_ref[...])
    stage_ref[:, D_MODEL:] = _bdot(p, ws_ref[...])

    def fix(e, _):
        cnt = cnt_ref[i * N_EXPERTS + e]

        @pl.when(cnt > 0)
        def _():
            r0 = pl.multiple_of(srow_ref[i * N_EXPERTS + e], 8)
            end = a_ref[i * N_EXPERTS + e] + cnt
            stage_ref[pl.ds(r0, 8), :] = stage_ref[pl.ds(r0, 8), :] + carry_ref[e]
            q = pl.multiple_of(jnp.minimum(r0 + end // 8 * 8, MOE_STAGE_ROWS - 8), 8)
            carry_ref[e] = jnp.where(end % 8 > 0, stage_ref[pl.ds(q, 8), :], 0.0)

        return 0

    lax.fori_loop(0, N_EXPERTS, fix, 0)

    def copy(s):
        src = stage_ref.at[pl.ds(pl.multiple_of(s * MOE_CHUNK, MOE_CHUNK), MOE_CHUNK), :]
        dst = xw_ref.at[pl.ds(pl.multiple_of(dst_ref[i * MOE_SLOTS + s], 8), MOE_CHUNK), :]
        return pltpu.make_async_copy(src, dst, sem_ref.at[s])

    def start(s, _):
        copy(s).start()
        return 0

    def wait(s, _):
        copy(s).wait()
        return 0

    lax.fori_loop(0, nslot_ref[i], start, 0)
    lax.fori_loop(0, nslot_ref[i], wait, 0)


def _scatter(plan, h2, ws, codet, xw_zero):
    npad = h2.shape[0]
    tm = TOKEN_TILE
    tile = lambda w: pl.BlockSpec((tm, w), lambda i, *_: (i, 0))
    return pl.pallas_call(
        _scatter_kernel,
        grid_spec=pltpu.PrefetchScalarGridSpec(
            num_scalar_prefetch=5, grid=(npad // tm,),
            in_specs=[tile(D_MODEL), tile(LANES),
                      pl.BlockSpec((N_EXPERTS, tm), lambda i, *_: (0, i)),
                      pl.BlockSpec((None, N_EXPERTS, 1), lambda i, *_: (i, 0, 0)),
                      pl.BlockSpec(memory_space=pl.ANY)],
            out_specs=pl.BlockSpec(memory_space=pl.ANY),
            scratch_shapes=[pltpu.VMEM((MOE_STAGE_ROWS, XW_COLS), F32),
                            pltpu.VMEM((N_EXPERTS, 8, XW_COLS), F32),
                            pltpu.SemaphoreType.DMA((MOE_SLOTS,))]),
        out_shape=jax.ShapeDtypeStruct(xw_zero.shape, F32),
        input_output_aliases={9: 0},
        compiler_params=_cparams(("arbitrary",)),
        name="moe_scatter",
    )(plan['nslot'], plan['dst'], plan['srow'], plan['a'], plan['cnt'], h2, ws, codet, plan['vect'], xw_zero)


def _expert_kernel(be_ref, nu_ref, xw_ref, w1_ref, b1_ref, w2_ref, b2_ref, y_ref, w1b_ref, w2b_ref):
    i = pl.program_id(0)
    e = be_ref[i]

    @pl.when(i < nu_ref[0])
    def _():
        @pl.when((i == 0) | (be_ref[jnp.maximum(i - 1, 0)] != e))
        def _():
            w1b_ref[...] = w1_ref[...].astype(BF16)
            w2b_ref[...] = w2_ref[...].astype(BF16)

        xw = xw_ref[...]
        wl = xw[:, D_MODEL:]
        lane = lax.broadcasted_iota(I32, wl.shape, 1)
        pick = ((lane & (N_EXPERTS - 1)) == e) & (lane < 3 * N_EXPERTS)
        wrow = jnp.sum(jnp.where(pick, wl, 0.0), axis=1, keepdims=True)
        u = _bdot(xw[:, :D_MODEL].astype(BF16), w1b_ref[...]) + b1_ref[...]
        g = jnp.minimum(u[:, :D_FF], SWIGLU_LIMIT)
        lin = jnp.clip(u[:, D_FF:], -SWIGLU_LIMIT, SWIGLU_LIMIT)
        act = g * jax.nn.sigmoid(SWIGLU_ALPHA * g) * (lin + 1.0)
        y = _bdot(act.astype(BF16), w2b_ref[...]) + b2_ref[...]
        y_ref[...] = y * wrow

    @pl.when(i >= nu_ref[0])
    def _():
        y_ref[...] = jnp.zeros_like(y_ref)


def _experts(block_e, n_used, xw, w1, b1, w2, b2):
    rmax = xw.shape[0]
    nblk = rmax // MOE_BLOCK

    def row_map(i, be, nu):
        return (jnp.minimum(i, nu[0] - 1), 0)

    return pl.pallas_call(
        _expert_kernel,
        grid_spec=pltpu.PrefetchScalarGridSpec(
            num_scalar_prefetch=2, grid=(nblk,),
            in_specs=[pl.BlockSpec((MOE_BLOCK, XW_COLS), row_map),
                      pl.BlockSpec((None, D_MODEL, 2 * D_FF), lambda i, be, nu: (be[i], 0, 0)),
                      pl.BlockSpec((None, 1, 2 * D_FF), lambda i, be, nu: (be[i], 0, 0)),
                      pl.BlockSpec((None, D_FF, D_MODEL), lambda i, be, nu: (be[i], 0, 0)),
                      pl.BlockSpec((None, 1, D_MODEL), lambda i, be, nu: (be[i], 0, 0))],
            out_specs=pl.BlockSpec((MOE_BLOCK, D_MODEL), lambda i, be, nu: (i, 0)),
            scratch_shapes=[pltpu.VMEM((D_MODEL, 2 * D_FF), BF16), pltpu.VMEM((D_FF, D_MODEL), BF16)]),
        out_shape=jax.ShapeDtypeStruct((rmax, D_MODEL), F32),
        compiler_params=_cparams(("arbitrary",)),
        name="moe_experts",
    )(block_e, n_used, xw, w1, b1, w2, b2)


def _combine_kernel(nslot_ref, dst_ref, code_ref, vec_ref, xmid_ref, gf_ref, y_ref, o_ref, ybuf_ref, sem_ref):
    i = pl.program_id(0)
    tm = code_ref.shape[0]
    nslot = nslot_ref[i]

    def copy(s):
        src = y_ref.at[pl.ds(pl.multiple_of(dst_ref[i * MOE_SLOTS + s], 8), MOE_CHUNK), :]
        dst = ybuf_ref.at[pl.ds(pl.multiple_of(s * MOE_CHUNK, MOE_CHUNK), MOE_CHUNK), :]
        return pltpu.make_async_copy(src, dst, sem_ref.at[s])

    def start(s, _):
        copy(s).start()
        return 0

    def clear(s, _):
        ybuf_ref[pl.ds(pl.multiple_of(s * MOE_CHUNK, MOE_CHUNK), MOE_CHUNK), :] = jnp.zeros((MOE_CHUNK, D_MODEL), F32)
        return 0

    def wait(s, _):
        copy(s).wait()
        return 0

    lax.fori_loop(0, nslot, start, 0)
    lax.fori_loop(nslot, MOE_SLOTS, clear, 0)
    cols = lax.broadcasted_iota(I32, (tm, MOE_STAGE_ROWS), 1)
    pt = jnp.zeros((tm, MOE_STAGE_ROWS), F32)
    for dest in _stage_dest(code_ref[...], vec_ref[...], 1):
        pt = jnp.where(cols == dest, 1.0, pt)
    lax.fori_loop(0, nslot, wait, 0)
    acc = _bdot(pt.astype(BF16), ybuf_ref[...].astype(BF16))
    o_ref[...] = xmid_ref[...] + gf_ref[...] * acc


def _combine(plan, code, xmid, gf, ypad):
    npad = xmid.shape[0]
    tm = TOKEN_TILE
    tile = lambda w: pl.BlockSpec((tm, w), lambda i, *_: (i, 0))
    return pl.pallas_call(
        _combine_kernel,
        grid_spec=pltpu.PrefetchScalarGridSpec(
            num_scalar_prefetch=2, grid=(npad // tm,),
            in_specs=[tile(LANES),
                      pl.BlockSpec((None, 1, LANES), lambda i, *_: (i, 0, 0)),
                      tile(D_MODEL), tile(D_MODEL),
                      pl.BlockSpec(memory_space=pl.ANY)],
            out_specs=tile(D_MODEL),
            scratch_shapes=[pltpu.VMEM((MOE_STAGE_ROWS, D_MODEL), F32),
                            pltpu.SemaphoreType.DMA((MOE_SLOTS,))]),
        out_shape=jax.ShapeDtypeStruct((npad, D_MODEL), F32),
        compiler_params=_cparams(("arbitrary",)),
        name="moe_combine",
    )(plan['nslot'], plan['dst'], code, plan['vec'], xmid, gf, ypad)


def _moe_plan(base, total):
    ntiles = base.shape[0]
    base = base[:, 0, :N_EXPERTS]
    counts = total[0, :N_EXPERTS]
    cap = (counts + MOE_CHUNK + MOE_BLOCK - 1) // MOE_BLOCK * MOE_BLOCK
    pend = jnp.cumsum(cap)
    pstart = pend - cap
    cnt = jnp.concatenate([base[1:], counts[None]], axis=0) - base
    off = pstart[None, :] + base
    a = off % 8
    nch = jnp.where(cnt > 0, (a + cnt + MOE_CHUNK - 1) // MOE_CHUNK, 0)
    cum = jnp.cumsum(nch, axis=1)
    slot0 = cum - nch
    srow = slot0 * MOE_CHUNK
    vec = srow + a
    s = jnp.arange(MOE_SLOTS)
    e_of_s = jnp.minimum(jnp.sum(s[None, :, None] >= cum[:, None, :], axis=2), N_EXPERTS - 1)
    c_of_s = s[None, :] - jnp.take_along_axis(slot0, e_of_s, axis=1)
    dst = jnp.take_along_axis(off - a, e_of_s, axis=1) + c_of_s * MOE_CHUNK
    dst = jnp.where(s[None, :] < cum[:, -1:], dst, 0)
    rmax = (ntiles * TOKEN_TILE * TOP_K + N_EXPERTS * (MOE_CHUNK + MOE_BLOCK)) // MOE_BLOCK * MOE_BLOCK
    n_used = pend[-1] // MOE_BLOCK
    blk = jnp.arange(rmax // MOE_BLOCK)
    block_e = jnp.searchsorted(pend, jnp.minimum(blk, n_used - 1) * MOE_BLOCK, side='right')
    block_e = jnp.minimum(block_e, N_EXPERTS - 1)
    flat = lambda v: v.reshape(-1).astype(I32)
    pad_lanes = jnp.zeros((ntiles, LANES - N_EXPERTS), I32)
    return dict(nslot=flat(cum[:, -1]), dst=flat(dst), srow=flat(srow), a=flat(a), cnt=flat(cnt),
                vect=vec.astype(I32)[:, :, None],
                vec=jnp.concatenate([vec.astype(I32), pad_lanes], axis=1)[:, None, :],
                block_e=block_e.astype(I32), n_used=n_used.reshape(1).astype(I32), rmax=rmax)


def _final_kernel(x_ref, g_ref, o_ref):
    o_ref[...] = _rms(x_ref[...], g_ref[...])


def _final_norm(xall, g):
    npad = xall.shape[0]
    tm = TOKEN_TILE
    return pl.pallas_call(
        _final_kernel,
        grid=(npad // tm,),
        in_specs=[pl.BlockSpec((tm, D_MODEL), lambda i: (i, 0)), pl.BlockSpec((1, D_MODEL), lambda i: (0, 0))],
        out_specs=pl.BlockSpec((tm, D_MODEL), lambda i: (i, 0)),
        out_shape=jax.ShapeDtypeStruct((npad, D_MODEL), F32),
        compiler_params=_cparams(("parallel",)),
        name="final_norm",
    )(xall, g)


def _rot_cols(w):
    half = w.shape[-1] // 2
    return jnp.concatenate([-w[..., half:], w[..., :half]], axis=-1)


def _layer_weights(l, w_in, mla_w_uq, mla_w_uk, mla_w_uv, w_branch, w_out, router_w, router_b):
    idx = [0]
    for s in W_IN_SIZES:
        idx.append(idx[-1] + s)
    w = w_in[l]
    kr = w[:, idx[2]:idx[3]]
    seg0 = jnp.concatenate([w[:, :idx[3]], _rot_cols(kr), jnp.zeros((D_MODEL, 64), F32)], axis=1)
    w_ext = jnp.concatenate([seg0, w[:, idx[3]:]], axis=1).astype(BF16)
    uq = mla_w_uq[l].reshape(MLA_Q_LORA, MLA_HEADS, MLA_NOPE + MLA_ROPE)
    nope = uq[:, :, :MLA_NOPE].reshape(MLA_Q_LORA, -1)
    pe = uq[:, :, MLA_NOPE:]
    wuq = jnp.concatenate([nope, pe.reshape(MLA_Q_LORA, -1), _rot_cols(pe).reshape(MLA_Q_LORA, -1)], axis=1).astype(BF16)
    wuk = jnp.transpose(mla_w_uk[l], (1, 2, 0)).astype(BF16)
    wuv = jnp.transpose(mla_w_uv[l], (1, 0, 2))
    z = jnp.zeros_like(wuv[0])
    wuv_bd = jnp.stack([jnp.concatenate([jnp.concatenate([wuv[2 * p], z], axis=1),
                                         jnp.concatenate([z, wuv[2 * p + 1]], axis=1)], axis=0)
                        for p in range(MLA_HEADS // 2)]).astype(BF16)
    wuv_all = mla_w_uv[l].reshape(MLA_KV_LORA, BRANCH_W).astype(BF16)
    rw = jnp.concatenate([router_w[l], jnp.zeros((D_MODEL, LANES - N_EXPERTS), F32)], axis=1).astype(BF16)
    rb = jnp.concatenate([router_b[l], jnp.full((LANES - N_EXPERTS,), NEG, F32)])[None]
    return dict(w_ext=w_ext, wuq=wuq, wuk=wuk, wuv_bd=wuv_bd, wuv_all=wuv_all,
                wbr=w_branch[l].astype(BF16), wout=w_out[l].astype(BF16), rw=rw, rb=rb)


def _rope_tables(pos):
    half = MLA_ROPE // 2
    inv = ROPE_THETA ** (-jnp.arange(half, dtype=F32) / half)
    ang = pos.astype(F32)[:, None] * inv[None, :]
    cos = jnp.tile(jnp.cos(ang), (1, 2 * MLA_HEADS))
    sin = jnp.tile(jnp.sin(ang), (1, 2 * MLA_HEADS))
    return cos, sin


def _pad_rows(a, rows):
    return jnp.concatenate([a, jnp.zeros((rows - a.shape[0],) + a.shape[1:], a.dtype)], axis=0)


def kernel(x_prompt, x_sample, c_prompt, c_sample, cache_mla_ckv, cache_mla_krope, cache_sb_k, cache_sb_v, state_hgrn, page_table, ada_w, ada_b, norm_mix, norm_ffn, w_in, mla_q_norm, mla_kv_norm, mla_w_uq, mla_w_uk, mla_w_uv, hg_lower_bounds, hg_norm, w_branch, w_out, router_w, router_b, exp_w1, exp_b1, exp_w2, exp_b2, final_norm):
    bp, tp, d = x_prompt.shape
    bs, ts, _ = x_sample.shape
    depth = w_in.shape[0]
    tm = TOKEN_TILE
    n_p, n_s = bp * tp, bs * ts
    assert d == D_MODEL and tp % tm == 0 and n_s <= tm and tp % MLA_TK == 0
    npg = page_table.shape[1]
    assert npg % PAGES_PER_STEP == 0 and cache_mla_ckv.shape[2] == PAGE_SIZE
    past_len = npg * PAGE_SIZE
    nt_p = tp // tm
    tile_s = n_p // tm
    npad = n_p + tm
    ts_pad = HG_CHUNK

    xall = jnp.concatenate([x_prompt.reshape(n_p, d), _pad_rows(x_sample.reshape(n_s, d), tm)], axis=0)
    mod = _adaln(jnp.concatenate([c_prompt, c_sample], axis=0), ada_w, ada_b)
    lower = _lower_bound(hg_lower_bounds)
    cos_p, sin_p = _rope_tables(jnp.arange(tp))
    cos_s, sin_s = _rope_tables(past_len + jnp.arange(ts))
    cos_s = _pad_rows(jnp.tile(cos_s, (bs, 1)), tm)
    sin_s = _pad_rows(jnp.tile(sin_s, (bs, 1)), tm)

    r = jnp.arange(SB_T)
    u_sb = (r[:, None] >= r[None, :]).astype(BF16)
    u_page = u_sb[:PAGE_SIZE, :PAGE_SIZE]
    head_mask = (jnp.arange(BRANCH_W)[None, :] // SB_DIM == jnp.arange(SB_HEADS)[:, None]).astype(F32)
    cache_k = cache_sb_k.reshape(cache_sb_k.shape[:3] + (BRANCH_W,))
    cache_v = cache_sb_v.reshape(cache_sb_v.shape[:3] + (BRANCH_W,))
    zero_state = jnp.zeros((bp, HG_HEADS, HG_DK, HG_DV), F32)

    def prompt_mod(v):
        return v[:bp, None, :]

    def sample_mod(v):
        return _pad_rows(jnp.repeat(v[bp:], ts, axis=0), tm)[None]

    def all_rows(v):
        return jnp.concatenate([jnp.repeat(v[:bp], tp, axis=0), sample_mod(v)[0]], axis=0)

    outs = [[] for _ in range(10)]
    for l in range(depth):
        lw = _layer_weights(l, w_in, mla_w_uq, mla_w_uk, mla_w_uv, w_branch, w_out, router_w, router_b)
        sh_m, sc_m, g_m, sh_f, sc_f, g_f = jnp.split(mod[l], 6, axis=-1)
        gn = norm_mix[l][None]
        qn = mla_q_norm[l][None]
        kvn = mla_kv_norm[l][None]
        lb = lower[l][None]

        (qcat, kcat, ckv, kr, sq, skb, svb, sk, sv, hq, hk, hv, hg, hgate, sig) = _mixer_in(
            xall, 0, bp, nt_p, gn, prompt_mod(sh_m), prompt_mod(sc_m), lw['w_ext'], qn, kvn,
            lw['wuq'], lw['wuk'], cos_p, sin_p, lb)
        o_mla = _mla_prompt(qcat, kcat, lw['wuv_bd'], bp, tp)
        o_sb = _sb_prompt(sq, skb, svb, u_sb, bp, tp)
        o_hg, st_p = _hgrn(hq, hk, hv, hg, zero_state, bp, tp)
        for lst, val in zip(outs[:5], (ckv.reshape(bp, tp, -1), kr.reshape(bp, tp, -1),
                                       sk.reshape(bp, tp, SB_HEADS, SB_DIM), sv.reshape(bp, tp, SB_HEADS, SB_DIM),
                                       st_p)):
            lst.append(val)

        (qcat_s, kcat_s, ckv_s, kr_s, sq_s, skb_s, svb_s, sk_s, sv_s, hq_s, hk_s, hv_s, hg_s, hgate_s, sig_s) = _mixer_in(
            xall, tile_s, 1, 1, gn, sample_mod(sh_m), sample_mod(sc_m), lw['w_ext'], qn, kvn,
            lw['wuq'], lw['wuk'], cos_s, sin_s, lb)
        q_s = jnp.transpose(qcat_s[:, :n_s].reshape(MLA_HEADS, bs, ts, MLA_CAT), (1, 2, 0, 3)).reshape(bs, ts * MLA_HEADS, MLA_CAT)
        kn_s = jnp.concatenate([kcat_s[:n_s].reshape(bs, ts, MLA_CAT),
                                jnp.zeros((bs, 16 - ts, MLA_CAT), BF16)], axis=1)
        o_mla_s = _mla_sample(l, page_table, q_s, kn_s, cache_mla_ckv, cache_mla_krope, lw['wuv_all'], head_mask)
        q_bd = (sq_s[:n_s].reshape(bs, ts, 1, BRANCH_W) * head_mask.astype(BF16)[None, None]).reshape(bs, ts * SB_HEADS, BRANCH_W)
        pad_new = lambda a: jnp.concatenate([a[:n_s].reshape(bs, ts, BRANCH_W),
                                             jnp.zeros((bs, PAGE_SIZE - ts, BRANCH_W), BF16)], axis=1)
        o_sb_s = _sb_sample(l, page_table, q_bd, pad_new(skb_s), pad_new(svb_s), cache_k, cache_v, u_page, head_mask)
        pad_t = lambda a: jnp.concatenate([a[:n_s].reshape(bs, ts, 512),
                                           jnp.zeros((bs, ts_pad - ts, 512), F32)], axis=1).reshape(bs * ts_pad, 512)
        o_hg_s, st_s = _hgrn(pad_t(hq_s), pad_t(hk_s), pad_t(hv_s), pad_t(hg_s), state_hgrn[l], bs, ts_pad)
        o_hg_s = o_hg_s.reshape(bs, ts_pad, 512)[:, :ts].reshape(n_s, 512)
        out_weights = (hg_norm[l][None], lw['wbr'], lw['wout'], norm_ffn[l][None], lw['rw'], lw['rb'])
        tail = _mixer_out(
            xall, tile_s, 1, 1, _pad_rows(o_mla_s.reshape(n_s, 512), tm), _pad_rows(o_sb_s.reshape(n_s, 512), tm),
            _pad_rows(o_hg_s, tm), hgate_s, sig_s, sample_mod(g_m), sample_mod(sh_f), sample_mod(sc_f),
            *out_weights, None)
        xmid, h2, logits = _mixer_out(xall, 0, bp, nt_p, o_mla, o_sb, o_hg, hgate, sig,
                                      prompt_mod(g_m), prompt_mod(sh_f), prompt_mod(sc_f), *out_weights, tail)
        for lst, val in zip(outs[5:], (ckv_s[:n_s].reshape(bs, ts, -1), kr_s[:n_s].reshape(bs, ts, -1),
                                       sk_s[:n_s].reshape(bs, ts, SB_HEADS, SB_DIM),
                                       sv_s[:n_s].reshape(bs, ts, SB_HEADS, SB_DIM), st_s)):
            lst.append(val)

        ws, code, codet, base, total = _route(logits, n_p + n_s)
        plan = _moe_plan(base, total)
        xw = _scatter(plan, h2, ws, codet, jnp.zeros((plan['rmax'], XW_COLS), F32))
        ypad = _experts(plan['block_e'], plan['n_used'], xw, exp_w1[l], exp_b1[l][:, None, :],
                        exp_w2[l], exp_b2[l][:, None, :])
        xall = _combine(plan, code, xmid, all_rows(g_f), ypad)

    y = _final_norm(xall, final_norm[None])
    y_prompt = y[:n_p].reshape(bp, tp, d)
    y_sample = y[n_p:n_p + n_s].reshape(bs, ts, d)
    st = [jnp.stack(o) for o in outs]
    return (y_prompt, y_sample, st[0], st[1], st[2], st[3], st[4], st[5], st[6], st[7], st[8], st[9])
```

```python
import functools

import jax
import jax.numpy as jnp
from jax import lax
from jax.experimental import pallas as pl
from jax.experimental.pallas import tpu as pltpu

F32 = jnp.float32
BF16 = jnp.bfloat16
I32 = jnp.int32

D_MODEL = 1024
MLA_HEADS, MLA_Q_LORA, MLA_KV_LORA, MLA_NOPE, MLA_ROPE, MLA_V = 8, 256, 128, 64, 32, 64
MLA_CAT = MLA_KV_LORA + MLA_ROPE
ROPE_THETA = 10000.0
MLA_SCALE = (MLA_NOPE + MLA_ROPE) ** -0.5
SB_HEADS, SB_DIM = 8, 64
SB_SCALE = SB_DIM ** -0.5
HG_HEADS, HG_DK, HG_DV = 4, 128, 128
BRANCH_W = 512
N_BRANCH = 3
N_EXPERTS, TOP_K, D_FF = 32, 4, 1024
SWIGLU_LIMIT, SWIGLU_ALPHA = 7.0, 1.702
PAGE_SIZE = 128
EPS = 1e-6

V7X_VMEM_BYTES = 64 * 1024 * 1024
VMEM_LIMIT = V7X_VMEM_BYTES * 7 // 8
LANES = 128
TOKEN_TILE = 256
MLA_TQ, MLA_TK = 128, 256
SB_T = 256
SB_PROMPT_HEADS = 4
HG_CHUNK = 16
MLA_PAGES = 32
SB_PAGES = 16
MOE_CHUNK = 16
MOE_BLOCK = 256
MOE_SLOTS = -(-(N_EXPERTS + (TOKEN_TILE * TOP_K + N_EXPERTS * 7) // MOE_CHUNK + 1) // 8) * 8
MOE_STAGE_ROWS = MOE_SLOTS * MOE_CHUNK
ROUTE_K_SHIFT = 10
ROUTE_K_STRIDE = 1 << ROUTE_K_SHIFT
XW_COLS = D_MODEL + LANES
NEG = -1e30

W_IN_SIZES = (MLA_Q_LORA, MLA_KV_LORA, MLA_ROPE, BRANCH_W, BRANCH_W, BRANCH_W,
              HG_HEADS * HG_DK, HG_HEADS * HG_DK, BRANCH_W, BRANCH_W, N_BRANCH * D_MODEL)


def _cparams(sem):
    return pltpu.CompilerParams(dimension_semantics=sem, vmem_limit_bytes=VMEM_LIMIT)


def _bdot(a, b):
    return jnp.dot(a, b, preferred_element_type=F32)


def _dot_nt(a, b):
    return lax.dot_general(a, b, (((1,), (1,)), ((), ())), preferred_element_type=F32)


def _dot_tn(a, b):
    return lax.dot_general(a, b, (((0,), (0,)), ((), ())), preferred_element_type=F32)


def _rms(x, g):
    return x * lax.rsqrt(jnp.mean(x * x, axis=-1, keepdims=True) + EPS) * g


def _split2(x):
    hi = x.astype(BF16)
    lo = (x - hi.astype(F32)).astype(BF16)
    return hi, lo


def _split3(x):
    hi = x.astype(BF16)
    r = x - hi.astype(F32)
    mid = r.astype(BF16)
    lo = (r - mid.astype(F32)).astype(BF16)
    return hi, mid, lo


def _adaln_kernel(c_ref, w_ref, b_ref, o_ref):
    c = c_ref[...]
    s = (c * jax.nn.sigmoid(c)).astype(BF16)
    o_ref[...] = _bdot(s, w_ref[...].astype(BF16)) + b_ref[...]


def _adaln(c_all, ada_w, ada_b):
    depth, d, six_d = ada_w.shape
    nc = c_all.shape[0]
    return pl.pallas_call(
        _adaln_kernel,
        grid=(depth, six_d // d),
        in_specs=[pl.BlockSpec((nc, d), lambda l, j: (0, 0)),
                  pl.BlockSpec((None, d, d), lambda l, j: (l, 0, j)),
                  pl.BlockSpec((None, 1, d), lambda l, j: (l, 0, j))],
        out_specs=pl.BlockSpec((None, nc, d), lambda l, j: (l, 0, j)),
        out_shape=jax.ShapeDtypeStruct((depth, nc, six_d), F32),
        compiler_params=_cparams(("parallel", "parallel")),
        name="adaln",
    )(c_all, ada_w, ada_b.reshape(depth, 1, six_d))


def _lower_bound_kernel(x_ref, o_ref):
    x = x_ref[...]
    e = jnp.exp(x - jnp.max(x, axis=0, keepdims=True))
    sm = e / jnp.sum(e, axis=0, keepdims=True)
    acc = jnp.zeros_like(sm[0:1])
    for l in range(x.shape[0]):
        acc = acc + sm[l:l + 1]
        o_ref[l:l + 1, :] = acc - sm[0:1]


def _lower_bound(hg_lower_bounds):
    return pl.pallas_call(
        _lower_bound_kernel,
        out_shape=jax.ShapeDtypeStruct(hg_lower_bounds.shape, F32),
        name="lower_bound",
    )(hg_lower_bounds)


def _mixer_in_kernel(x_ref, gn_ref, sh_ref, sc_ref, w_ref, qn_ref, kvn_ref, wuq_ref, wuk_ref,
                     cos_ref, sin_ref, lb_ref,
                     qcat_ref, kcat_ref, ckv_ref, kr_ref, sq_ref, skb_ref, svb_ref, sk_ref, sv_ref,
                     hq_ref, hk_ref, hv_ref, hg_ref, hgate_ref, sig_ref):
    x = x_ref[...]
    h = (_rms(x, gn_ref[...]) * (1.0 + sc_ref[...]) + sh_ref[...]).astype(BF16)

    def seg(start, width):
        return _bdot(h, w_ref[:, start:start + width])

    u0 = seg(0, 512)
    cq = u0[:, :MLA_Q_LORA]
    ckv = _rms(u0[:, 256:384], kvn_ref[...])
    cos8 = cos_ref[...]
    sin8 = sin_ref[...]
    kr = u0[:, 384:416] * cos8[:, :MLA_ROPE] + u0[:, 416:448] * sin8[:, :MLA_ROPE]
    ckv_ref[...] = ckv
    kr_ref[...] = kr
    kcat_ref[:, :MLA_KV_LORA] = ckv.astype(BF16)
    kcat_ref[:, MLA_KV_LORA:] = kr.astype(BF16)

    qq = _bdot(_rms(cq, qn_ref[...]).astype(BF16), wuq_ref[...])
    pe = qq[:, 512:768] * cos8 + qq[:, 768:1024] * sin8
    for hd in range(MLA_HEADS):
        qn = qq[:, hd * MLA_NOPE:(hd + 1) * MLA_NOPE].astype(BF16)
        qcat_ref[hd, :, :MLA_KV_LORA] = _bdot(qn, wuk_ref[hd]).astype(BF16)
        qcat_ref[hd, :, MLA_KV_LORA:] = pe[:, hd * MLA_ROPE:(hd + 1) * MLA_ROPE].astype(BF16)

    sq_ref[...] = seg(512, 512).astype(BF16)
    sk = seg(1024, 512)
    sk_ref[...] = sk
    skb_ref[...] = sk.astype(BF16)
    sv = seg(1536, 512)
    sv_ref[...] = sv
    svb_ref[...] = sv.astype(BF16)
    lb = lb_ref[...]
    f = lb + (1.0 - lb) * jax.nn.sigmoid(seg(2048, 512))
    hg_ref[...] = jnp.log(f)
    hk_ref[...] = 1.0 - f
    hq_ref[...] = seg(2560, 512)
    hv_ref[...] = seg(3072, 512)
    hgate_ref[...] = seg(3584, 512)
    for n in range(N_BRANCH):
        c0 = 4096 + n * D_MODEL
        sig_ref[:, n * D_MODEL:(n + 1) * D_MODEL] = jax.nn.sigmoid(seg(c0, D_MODEL)).astype(BF16)


def _mixer_in(xall, tile0, nb, nt, gn, sh, sc, w_ext, qn, kvn, wuq, wuk, cos8, sin8, lb):
    tm = TOKEN_TILE
    rows = nb * nt * tm
    rmod = sh.shape[1]
    row_spec = lambda w: pl.BlockSpec((tm, w), lambda b, i: (b * nt + i, 0))
    full = lambda a: pl.BlockSpec(a.shape, lambda b, i: (0,) * a.ndim)
    mod_spec = pl.BlockSpec((None, rmod, D_MODEL), lambda b, i: (b, 0, 0))
    pos_spec = pl.BlockSpec((tm, 256), lambda b, i: (i, 0))
    out_shapes = [
        jax.ShapeDtypeStruct((MLA_HEADS, rows, MLA_CAT), BF16),
        jax.ShapeDtypeStruct((rows, MLA_CAT), BF16),
        jax.ShapeDtypeStruct((rows, MLA_KV_LORA), F32),
        jax.ShapeDtypeStruct((rows, MLA_ROPE), F32),
        jax.ShapeDtypeStruct((rows, 512), BF16),
        jax.ShapeDtypeStruct((rows, 512), BF16),
        jax.ShapeDtypeStruct((rows, 512), BF16),
        jax.ShapeDtypeStruct((rows, 512), F32),
        jax.ShapeDtypeStruct((rows, 512), F32),
        jax.ShapeDtypeStruct((rows, 512), F32),
        jax.ShapeDtypeStruct((rows, 512), F32),
        jax.ShapeDtypeStruct((rows, 512), F32),
        jax.ShapeDtypeStruct((rows, 512), F32),
        jax.ShapeDtypeStruct((rows, 512), F32),
        jax.ShapeDtypeStruct((rows, N_BRANCH * D_MODEL), BF16),
    ]
    out_specs = [pl.BlockSpec((MLA_HEADS, tm, MLA_CAT), lambda b, i: (0, b * nt + i, 0))]
    out_specs += [row_spec(s.shape[1]) for s in out_shapes[1:]]
    return pl.pallas_call(
        _mixer_in_kernel,
        grid=(nb, nt),
        in_specs=[pl.BlockSpec((tm, D_MODEL), lambda b, i: (tile0 + b * nt + i, 0)),
                  full(gn), mod_spec, mod_spec, full(w_ext), full(qn), full(kvn), full(wuq), full(wuk),
                  pos_spec, pos_spec, full(lb)],
        out_specs=out_specs,
        out_shape=out_shapes,
        compiler_params=_cparams(("parallel", "parallel")),
        name="mixer_in",
    )(xall, gn, sh, sc, w_ext, qn, kvn, wuq, wuk, cos8, sin8, lb)


def _mla_prompt_kernel(q_ref, k_ref, wuv_ref, o_ref, *, tq, tk):
    i = pl.program_id(1)
    rows = MLA_HEADS * tq
    q = q_ref[...].reshape(rows, MLA_CAT)
    last = (i * tq + tq - 1) // tk

    def step(j, carry, masked):
        m, l, acc = carry
        k = k_ref[pl.ds(pl.multiple_of(j * tk, tk), tk), :]
        s = _dot_nt(q, k) * MLA_SCALE
        if masked:
            qpos = i * tq + (lax.broadcasted_iota(I32, (rows, tk), 0) & (tq - 1))
            kpos = j * tk + lax.broadcasted_iota(I32, (rows, tk), 1)
            s = jnp.where(kpos <= qpos, s, NEG)
        m_new = jnp.maximum(m, jnp.max(s, axis=1, keepdims=True))
        alpha = jnp.exp(m - m_new)
        p = jnp.exp(s - m_new)
        l = alpha * l + jnp.sum(p, axis=1, keepdims=True)
        acc = alpha * acc + _bdot(p.astype(BF16), k[:, :MLA_KV_LORA])
        return m_new, l, acc

    init = (jnp.full((rows, 1), NEG, F32), jnp.zeros((rows, 1), F32), jnp.zeros((rows, MLA_KV_LORA), F32))
    carry = lax.fori_loop(0, last, lambda j, c: step(j, c, False), init)
    _, l, acc = step(last, carry, True)
    o = (acc / l).astype(BF16)
    for hp in range(MLA_HEADS // 2):
        pair = jnp.concatenate([o[(2 * hp) * tq:(2 * hp + 1) * tq], o[(2 * hp + 1) * tq:(2 * hp + 2) * tq]], axis=1)
        o_ref[:, hp * LANES:(hp + 1) * LANES] = _bdot(pair, wuv_ref[hp]).astype(BF16)


def _mla_prompt(qcat, kcat, wuv_bd, nb, t):
    tq, tk = MLA_TQ, MLA_TK
    nq = t // tq
    return pl.pallas_call(
        functools.partial(_mla_prompt_kernel, tq=tq, tk=tk),
        grid=(nb, nq),
        in_specs=[pl.BlockSpec((MLA_HEADS, tq, MLA_CAT), lambda b, i: (0, b * nq + i, 0)),
                  pl.BlockSpec((t, MLA_CAT), lambda b, i: (b, 0)),
                  pl.BlockSpec(wuv_bd.shape, lambda b, i: (0, 0, 0))],
        out_specs=pl.BlockSpec((tq, BRANCH_W), lambda b, i: (b * nq + i, 0)),
        out_shape=jax.ShapeDtypeStruct((nb * t, BRANCH_W), BF16),
        compiler_params=_cparams(("parallel", "parallel")),
        name="mla_prompt",
    )(qcat, kcat, wuv_bd)


def _neg_softplus(z):
    return -(jnp.maximum(z, 0.0) + jnp.log(1.0 + jnp.exp(-jnp.abs(z))))


def _sb_tiles(qs, ks, vs, u, rests, valid):
    zs = [_dot_nt(q, k) * SB_SCALE for q, k in zip(qs, ks)]
    lks = [_neg_softplus(z) for z in zs]
    if valid is not None:
        lks = [jnp.where(valid, lk, 0.0) for lk in lks]
    parts = [_split2(lk) for lk in lks]
    incls = [_bdot(hi, u) + _bdot(lo, u) for hi, lo in parts]
    ws = [jnp.exp(z + incl + rest) for z, incl, rest in zip(zs, incls, rests)]
    if valid is not None:
        ws = [jnp.where(valid, w, 0.0) for w in ws]
    outs = [_bdot(w.astype(BF16), v) for w, v in zip(ws, vs)]
    return [(o, rest + incl[:, 0:1]) for o, incl, rest in zip(outs, incls, rests)]


def _sb_prompt_kernel(q_ref, k_ref, v_ref, u_ref, o_ref, *, t, nh):
    i = pl.program_id(2)
    u = u_ref[...]
    q = q_ref[...]
    row = lax.broadcasted_iota(I32, (t, t), 0)
    col = lax.broadcasted_iota(I32, (t, t), 1)
    valid = col < row
    heads = [slice(hh * SB_DIM, (hh + 1) * SB_DIM) for hh in range(nh)]
    qs = [q[:, hs] for hs in heads]

    def block(j, carry, mask):
        r0 = pl.multiple_of(j * t, t)
        k = k_ref[pl.ds(r0, t), :]
        v = v_ref[pl.ds(r0, t), :]
        res = _sb_tiles(qs, [k[:, hs] for hs in heads], [v[:, hs] for hs in heads], u,
                        [c[1] for c in carry], mask)
        return tuple((c[0] + o, rest) for c, (o, rest) in zip(carry, res))

    zero = (jnp.zeros((t, SB_DIM), F32), jnp.zeros((t, 1), F32))
    carry = block(i, (zero,) * nh, valid)
    carry = lax.fori_loop(0, i, lambda jj, c: block(i - 1 - jj, c, None), carry)
    o_ref[...] = jnp.concatenate([c[0] for c in carry], axis=1).astype(BF16)


def _sb_prompt(sq, sk, sv, u, nb, t):
    tb = SB_T
    nq = t // tb
    nh = SB_PROMPT_HEADS
    width = nh * SB_DIM
    return pl.pallas_call(
        functools.partial(_sb_prompt_kernel, t=tb, nh=nh),
        grid=(nb, SB_HEADS // nh, nq),
        in_specs=[pl.BlockSpec((tb, width), lambda b, hg, i: (b * nq + i, hg)),
                  pl.BlockSpec((t, width), lambda b, hg, i: (b, hg)),
                  pl.BlockSpec((t, width), lambda b, hg, i: (b, hg)),
                  pl.BlockSpec((tb, tb), lambda b, hg, i: (0, 0))],
        out_specs=pl.BlockSpec((tb, width), lambda b, hg, i: (b * nq + i, hg)),
        out_shape=jax.ShapeDtypeStruct((nb * t, BRANCH_W), BF16),
        compiler_params=_cparams(("parallel", "parallel", "parallel")),
        name="sb_prompt",
    )(sq, sk, sv, u)


def _hgrn_kernel(q_ref, k_ref, v_ref, g_ref, s0_ref, l_ref, o_ref, sf_ref, st_ref, bc_ref, *, tt, nt):
    i = pl.program_id(1)
    c = HG_CHUNK

    @pl.when(i == 0)
    def _():
        for hd in range(HG_HEADS):
            st_ref[hd] = s0_ref[hd].T

    lmat = l_ref[...]
    g1, g2, g3 = _split3(g_ref[...])
    bc_ref[...] = _bdot(lmat, g1) + _bdot(lmat, g2) + _bdot(lmat, g3)
    row = lax.broadcasted_iota(I32, (c, HG_DK), 0)

    def chunk(ci, _):
        r0 = pl.multiple_of(ci * c, c)
        for hd in range(HG_HEADS):
            sl = (pl.ds(r0, c), slice(hd * HG_DK, (hd + 1) * HG_DK))
            bc = bc_ref[sl]
            q = q_ref[sl]
            k = k_ref[sl]
            v = v_ref[sl]
            st = st_ref[hd]
            inter = _dot_nt((q * jnp.exp(bc)).astype(BF16), st.astype(BF16))
            intra = jnp.zeros((c, HG_DV), F32)
            for s in range(c):
                decay = jnp.where(row >= s, jnp.exp(bc - bc[s:s + 1, :]), 0.0)
                att = jnp.sum(q * (k[s:s + 1, :] * decay), axis=1, keepdims=True)
                intra = intra + att * v[s:s + 1, :]
            o_ref[sl] = inter + intra
            bend = bc[c - 1:c, :]
            kend = k * jnp.exp(bend - bc)
            st_ref[hd] = st * jnp.exp(bend) + _dot_tn(v.astype(BF16), kend.astype(BF16))
        return 0

    lax.fori_loop(0, tt // c, chunk, 0)

    @pl.when(i == nt - 1)
    def _():
        for hd in range(HG_HEADS):
            sf_ref[hd] = st_ref[hd].T


def _hgrn(hq, hk, hv, hg, s0, nb, t):
    tt = min(t, TOKEN_TILE)
    nt = t // tt
    r = jnp.arange(tt)
    lmat = ((r[:, None] >= r[None, :]) & (r[:, None] // HG_CHUNK == r[None, :] // HG_CHUNK)).astype(BF16)
    row_spec = pl.BlockSpec((tt, 512), lambda b, i: (b * nt + i, 0))
    st_spec = pl.BlockSpec((None, HG_HEADS, HG_DK, HG_DV), lambda b, i: (b, 0, 0, 0))
    return pl.pallas_call(
        functools.partial(_hgrn_kernel, tt=tt, nt=nt),
        grid=(nb, nt),
        in_specs=[row_spec, row_spec, row_spec, row_spec, st_spec,
                  pl.BlockSpec((tt, tt), lambda b, i: (0, 0))],
        out_specs=[row_spec, st_spec],
        out_shape=[jax.ShapeDtypeStruct((nb * t, 512), F32),
                   jax.ShapeDtypeStruct((nb, HG_HEADS, HG_DK, HG_DV), F32)],
        scratch_shapes=[pltpu.VMEM((HG_HEADS, HG_DV, HG_DK), F32), pltpu.VMEM((tt, 512), F32)],
        compiler_params=_cparams(("parallel", "arbitrary")),
        name="hgrn",
    )(hq, hk, hv, hg, s0, lmat)


def _diag_heads(res, mask):
    tdim = res.shape[0] // 8
    return jnp.sum(res.reshape(tdim, 8, BRANCH_W) * mask[None], axis=1)


def _page_pipeline(pt_ref, caches, bufs, sem_ref, *, layer, npg, npp, nsteps, newest_first):
    b = pl.program_id(0)
    j = pl.program_id(1)
    step = b * nsteps + j
    slot = step % 2

    def copies(bb, jj, sl):
        out = []
        for n in range(npp):
            logical = (npg - (jj + 1) * npp + n) if newest_first else (jj * npp + n)
            page = pt_ref[bb * npg + logical]
            for ci, (cache, buf) in enumerate(zip(caches, bufs)):
                out.append(pltpu.make_async_copy(cache.at[layer, page], buf.at[sl, n], sem_ref.at[ci, sl, n]))
        return out

    @pl.when(step == 0)
    def _():
        for cp in copies(b, j, slot):
            cp.start()

    @pl.when(step + 1 < pl.num_programs(0) * nsteps)
    def _():
        wrap = j + 1 == nsteps
        for cp in copies(jnp.where(wrap, b + 1, b), jnp.where(wrap, 0, j + 1), 1 - slot):
            cp.start()

    for cp in copies(b, j, slot):
        cp.wait()
    return slot


def _mla_sample_kernel(pt_ref, q_ref, kn_ref, ckv_hbm, kr_hbm, wuv_ref, mask_ref, o_ref,
                       ckv_buf, kr_buf, sem_ref, m_ref, l_ref, acc_ref, *, layer, npg, npp, nsteps, ts):
    j = pl.program_id(1)
    rows = ts * MLA_HEADS
    q = q_ref[...]
    ql = q[:, :MLA_KV_LORA]
    qp = q[:, MLA_KV_LORA:]
    slot = _page_pipeline(pt_ref, (ckv_hbm, kr_hbm), (ckv_buf, kr_buf), sem_ref, layer=layer, npg=npg,
                          npp=npp, nsteps=nsteps, newest_first=False)

    @pl.when(j == 0)
    def _():
        kn = kn_ref[...]
        nk = kn.shape[0]
        s = _dot_nt(q, kn) * MLA_SCALE
        trow = lax.broadcasted_iota(I32, (rows, nk), 0) // MLA_HEADS
        tcol = lax.broadcasted_iota(I32, (rows, nk), 1)
        s = jnp.where(tcol <= trow, s, NEG)
        m = jnp.max(s, axis=1, keepdims=True)
        p = jnp.exp(s - m)
        m_ref[...] = m
        l_ref[...] = jnp.sum(p, axis=1, keepdims=True)
        acc_ref[...] = _bdot(p.astype(BF16), kn[:, :MLA_KV_LORA])

    ck = ckv_buf[slot].reshape(npp * PAGE_SIZE, MLA_KV_LORA).astype(BF16)
    krt = jnp.concatenate([kr_buf[slot, n] for n in range(npp)], axis=1).astype(BF16)
    s = (_dot_nt(ql, ck) + _bdot(qp, krt)) * MLA_SCALE
    m_old = m_ref[...]
    m_new = jnp.maximum(m_old, jnp.max(s, axis=1, keepdims=True))
    alpha = jnp.exp(m_old - m_new)
    p = jnp.exp(s - m_new)
    m_ref[...] = m_new
    l_ref[...] = alpha * l_ref[...] + jnp.sum(p, axis=1, keepdims=True)
    acc_ref[...] = alpha * acc_ref[...] + _bdot(p.astype(BF16), ck)

    @pl.when(j == nsteps - 1)
    def _():
        o = (acc_ref[...] / l_ref[...]).astype(BF16)
        o_ref[...] = _diag_heads(_bdot(o, wuv_ref[...]), mask_ref[...]).astype(BF16)


def _mla_sample(layer, page_table, q_s, kn_s, cache_ckv, cache_kr, wuv_all, head_mask):
    bs, npg = page_table.shape
    npp = MLA_PAGES
    nsteps = npg // npp
    rows = q_s.shape[1]
    ts = rows // MLA_HEADS

    in_specs = [pl.BlockSpec((None, rows, MLA_CAT), lambda b, j, pt: (b, 0, 0)),
                pl.BlockSpec((None,) + kn_s.shape[1:], lambda b, j, pt: (b, 0, 0)),
                pl.BlockSpec(memory_space=pl.ANY), pl.BlockSpec(memory_space=pl.ANY),
                pl.BlockSpec(wuv_all.shape, lambda b, j, pt: (0, 0)),
                pl.BlockSpec(head_mask.shape, lambda b, j, pt: (0, 0))]
    return pl.pallas_call(
        functools.partial(_mla_sample_kernel, layer=layer, npg=npg, npp=npp, nsteps=nsteps, ts=ts),
        grid_spec=pltpu.PrefetchScalarGridSpec(
            num_scalar_prefetch=1, grid=(bs, nsteps), in_specs=in_specs,
            out_specs=pl.BlockSpec((None, ts, BRANCH_W), lambda b, j, pt: (b, 0, 0)),
            scratch_shapes=[pltpu.VMEM((2, npp, PAGE_SIZE, MLA_KV_LORA), F32),
                            pltpu.VMEM((2, npp, MLA_ROPE, PAGE_SIZE), F32),
                            pltpu.SemaphoreType.DMA((2, 2, npp)),
                            pltpu.VMEM((rows, 1), F32), pltpu.VMEM((rows, 1), F32),
                            pltpu.VMEM((rows, MLA_KV_LORA), F32)]),
        out_shape=jax.ShapeDtypeStruct((bs, ts, BRANCH_W), BF16),
        compiler_params=_cparams(("arbitrary", "arbitrary")),
        name="mla_sample",
    )(page_table.reshape(-1), q_s, kn_s, cache_ckv, cache_kr, wuv_all, head_mask)


def _sb_sample_kernel(pt_ref, q_ref, kn_ref, vn_ref, k_hbm, v_hbm, u_ref, mask_ref, o_ref,
                      k_buf, v_buf, sem_ref, rest_ref, acc_ref, *, layer, npg, npp, nsteps, ts):
    j = pl.program_id(1)
    rows = ts * SB_HEADS
    q = q_ref[...]
    u = u_ref[...]
    slot = _page_pipeline(pt_ref, (k_hbm, v_hbm), (k_buf, v_buf), sem_ref, layer=layer, npg=npg,
                          npp=npp, nsteps=nsteps, newest_first=True)

    @pl.when(j == 0)
    def _():
        nk = kn_ref.shape[0]
        trow = lax.broadcasted_iota(I32, (rows, nk), 0) // SB_HEADS
        tcol = lax.broadcasted_iota(I32, (rows, nk), 1)
        (o, r), = _sb_tiles([q], [kn_ref[...]], [vn_ref[...]], u, [jnp.zeros((rows, 1), F32)], tcol < trow)
        acc_ref[...] = o
        rest_ref[...] = r

    kt = jnp.concatenate([k_buf[slot, n] for n in range(npp)], axis=1).astype(BF16)
    vt = jnp.concatenate([v_buf[slot, n] for n in range(npp)], axis=1).astype(BF16)
    z = _bdot(q, kt) * SB_SCALE
    hi, lo = _split2(_neg_softplus(z))
    pages = [slice(n * PAGE_SIZE, (n + 1) * PAGE_SIZE) for n in range(npp)]
    stacked = jnp.concatenate([hi[:, p] for p in pages] + [lo[:, p] for p in pages], axis=0)
    cs = _bdot(stacked, u)
    incl = [cs[n * rows:(n + 1) * rows] + cs[(npp + n) * rows:(npp + n + 1) * rows] for n in range(npp)]
    r = rest_ref[...]
    shifted = [None] * npp
    for n in reversed(range(npp)):
        shifted[n] = incl[n] + r
        r = r + incl[n][:, 0:1]
    rest_ref[...] = r
    w = jnp.exp(z + jnp.concatenate(shifted, axis=1))
    acc_ref[...] = acc_ref[...] + _dot_nt(w.astype(BF16), vt)

    @pl.when(j == nsteps - 1)
    def _():
        o_ref[...] = _diag_heads(acc_ref[...], mask_ref[...]).astype(BF16)


def _sb_sample(layer, page_table, q_bd, kn, vn, cache_k, cache_v, u, head_mask):
    bs, npg = page_table.shape
    npp = SB_PAGES
    nsteps = npg // npp
    rows = q_bd.shape[1]
    ts = rows // SB_HEADS
    page_shape = (2, npp, BRANCH_W, PAGE_SIZE)

    new_spec = pl.BlockSpec((None, PAGE_SIZE, BRANCH_W), lambda b, j, pt: (b, 0, 0))
    in_specs = [pl.BlockSpec((None, rows, BRANCH_W), lambda b, j, pt: (b, 0, 0)), new_spec, new_spec,
                pl.BlockSpec(memory_space=pl.ANY), pl.BlockSpec(memory_space=pl.ANY),
                pl.BlockSpec(u.shape, lambda b, j, pt: (0, 0)),
                pl.BlockSpec(head_mask.shape, lambda b, j, pt: (0, 0))]
    return pl.pallas_call(
        functools.partial(_sb_sample_kernel, layer=layer, npg=npg, npp=npp, nsteps=nsteps, ts=ts),
        grid_spec=pltpu.PrefetchScalarGridSpec(
            num_scalar_prefetch=1, grid=(bs, nsteps), in_specs=in_specs,
            out_specs=pl.BlockSpec((None, ts, BRANCH_W), lambda b, j, pt: (b, 0, 0)),
            scratch_shapes=[pltpu.VMEM(page_shape, F32), pltpu.VMEM(page_shape, F32),
                            pltpu.SemaphoreType.DMA((2, 2, npp)),
                            pltpu.VMEM((rows, 1), F32), pltpu.VMEM((rows, BRANCH_W), F32)]),
        out_shape=jax.ShapeDtypeStruct((bs, ts, BRANCH_W), BF16),
        compiler_params=_cparams(("arbitrary", "arbitrary")),
        name="sb_sample",
    )(page_table.reshape(-1), q_bd, kn, vn, cache_k, cache_v, u, head_mask)


def _mixer_out_kernel(*refs, n_compute, has_tail):
    n_in = 15
    outs = refs[n_in + (3 if has_tail else 0):]
    if not has_tail:
        _mixer_out_tile(*refs[:n_in], *outs)
        return
    i = pl.program_id(0)

    @pl.when(i < n_compute)
    def _():
        _mixer_out_tile(*refs[:n_in], *outs)

    @pl.when(i == n_compute)
    def _():
        for src, dst in zip(refs[n_in:n_in + 3], outs):
            dst[...] = src[...]


def _mixer_out_tile(x_ref, omla_ref, osb_ref, ohg_ref, hgate_ref, sig_ref, gm_ref, shf_ref, scf_ref,
                    hgn_ref, wbr_ref, wout_ref, nf_ref, rw_ref, rb_ref,
                    xmid_ref, h2_ref, logit_ref):
    ohg = ohg_ref[...]
    hgn = hgn_ref[...]
    normed = jnp.concatenate(
        [_rms(ohg[:, hd * HG_DV:(hd + 1) * HG_DV], hgn) for hd in range(HG_HEADS)], axis=1)
    gate = hgate_ref[...]
    br_hg = (normed * (gate * jax.nn.sigmoid(gate))).astype(BF16)
    branches = (omla_ref[...], osb_ref[...], br_hg)
    merged = jnp.zeros((x_ref.shape[0], D_MODEL), F32)
    for n in range(N_BRANCH):
        sig = sig_ref[:, n * D_MODEL:(n + 1) * D_MODEL].astype(F32)
        merged = merged + sig * _bdot(branches[n], wbr_ref[n])
    x = x_ref[...] + gm_ref[...] * _bdot(merged.astype(BF16), wout_ref[...])
    xmid_ref[...] = x
    h2 = (_rms(x, nf_ref[...]) * (1.0 + scf_ref[...]) + shf_ref[...]).astype(BF16)
    h2_ref[...] = h2
    logit_ref[...] = _bdot(h2, rw_ref[...]) + rb_ref[...]


def _mixer_out(xall, tile0, nb, nt, omla, osb, ohg, hgate, sig, gm, shf, scf, hgn, wbr, wout, nf, rw, rb, tail):
    tm = TOKEN_TILE
    rmod = gm.shape[1]
    n_compute = nb * nt
    ntile = n_compute + (0 if tail is None else 1)
    last = n_compute - 1
    glob = lambda w: pl.BlockSpec((tm, w), lambda i: (tile0 + jnp.minimum(i, last), 0))
    loc = lambda w: pl.BlockSpec((tm, w), lambda i: (jnp.minimum(i, last), 0))
    out = lambda w: pl.BlockSpec((tm, w), lambda i: (i, 0))
    full = lambda a: pl.BlockSpec(a.shape, lambda i: (0,) * a.ndim)
    mod_spec = pl.BlockSpec((None, rmod, D_MODEL), lambda i: (jnp.minimum(i, last) // nt, 0, 0))
    in_specs = [glob(D_MODEL), loc(512), loc(512), loc(512), loc(512), loc(N_BRANCH * D_MODEL),
                mod_spec, mod_spec, mod_spec, full(hgn), full(wbr), full(wout), full(nf), full(rw), full(rb)]
    args = [xall, omla, osb, ohg, hgate, sig, gm, shf, scf, hgn, wbr, wout, nf, rw, rb]
    if tail is not None:
        in_specs += [full(t) for t in tail]
        args += list(tail)
    return pl.pallas_call(
        functools.partial(_mixer_out_kernel, n_compute=n_compute, has_tail=tail is not None),
        grid=(ntile,),
        in_specs=in_specs,
        out_specs=[out(D_MODEL), out(D_MODEL), out(LANES)],
        out_shape=[jax.ShapeDtypeStruct((ntile * tm, D_MODEL), F32),
                   jax.ShapeDtypeStruct((ntile * tm, D_MODEL), BF16),
                   jax.ShapeDtypeStruct((ntile * tm, LANES), F32)],
        compiler_params=_cparams(("parallel",)),
        name="mixer_out",
    )(*args)


def _route_kernel(logit_ref, lt_ref, ws_ref, lr_ref, lrt_ref, base_ref, tot_ref, carry_ref, *, n_valid, ntiles):
    i = pl.program_id(0)
    tm = TOKEN_TILE

    @pl.when(i == 0)
    def _():
        carry_ref[...] = jnp.zeros_like(carry_ref)

    x = logit_ref[...]
    lane = lax.broadcasted_iota(I32, (tm, LANES), 1)
    rowg = i * tm + lax.broadcasted_iota(I32, (tm, LANES), 0)
    sels, vals = [], []
    for _ in range(TOP_K):
        m = jnp.max(x, axis=1, keepdims=True)
        idx = jnp.min(jnp.where(x == m, lane, LANES), axis=1, keepdims=True)
        sel = lane == idx
        sels.append(sel)
        vals.append(m)
        x = jnp.where(sel, -jnp.inf, x)
    ex = [jnp.exp(v - vals[0]) for v in vals]
    den = ex[0] + ex[1] + ex[2] + ex[3]
    w = jnp.zeros((tm, LANES), F32)
    msk = jnp.zeros((tm, LANES), F32)
    kth = jnp.full((tm, LANES), -1.0, F32)
    live = rowg < n_valid
    for k, (sel, e) in enumerate(zip(sels, ex)):
        pick = sel & live
        w = jnp.where(pick, e / den, w)
        msk = jnp.where(pick, 1.0, msk)
        kth = jnp.where(pick, float(k), kth)
    hi, mid, lo = _split3(w)
    ws = hi.astype(F32) + pltpu.roll(mid.astype(F32), N_EXPERTS, 1) + pltpu.roll(lo.astype(F32), 2 * N_EXPERTS, 1)
    ws_ref[...] = ws.astype(BF16)
    rank = _bdot(lt_ref[...], msk.astype(BF16))
    code = jnp.where(msk > 0.0, rank + ROUTE_K_STRIDE * kth, -1.0)
    lr_ref[...] = code.astype(I32)
    lrt_ref[...] = code.T.astype(I32)
    base_ref[...] = carry_ref[...].astype(I32)
    carry_ref[...] = carry_ref[...] + jnp.sum(msk, axis=0, keepdims=True)

    @pl.when(i == ntiles - 1)
    def _():
        tot_ref[...] = carry_ref[...].astype(I32)


def _route(logits, n_valid):
    npad = logits.shape[0]
    tm = TOKEN_TILE
    ntiles = npad // tm
    r = jnp.arange(tm)
    lt = (r[:, None] > r[None, :]).astype(BF16)
    return pl.pallas_call(
        functools.partial(_route_kernel, n_valid=n_valid, ntiles=ntiles),
        grid=(ntiles,),
        in_specs=[pl.BlockSpec((tm, LANES), lambda i: (i, 0)),
                  pl.BlockSpec((tm, tm), lambda i: (0, 0))],
        out_specs=[pl.BlockSpec((tm, LANES), lambda i: (i, 0)),
                   pl.BlockSpec((tm, LANES), lambda i: (i, 0)),
                   pl.BlockSpec((LANES, tm), lambda i: (0, i)),
                   pl.BlockSpec((None, 1, LANES), lambda i: (i, 0, 0)),
                   pl.BlockSpec((1, LANES), lambda i: (0, 0))],
        out_shape=[jax.ShapeDtypeStruct((npad, LANES), BF16),
                   jax.ShapeDtypeStruct((npad, LANES), I32),
                   jax.ShapeDtypeStruct((LANES, npad), I32),
                   jax.ShapeDtypeStruct((ntiles, 1, LANES), I32),
                   jax.ShapeDtypeStruct((1, LANES), I32)],
        scratch_shapes=[pltpu.VMEM((1, LANES), F32)],
        compiler_params=_cparams(("arbitrary",)),
        name="route",
    )(logits, lt)


def _stage_dest(code, vec, axis):
    rank = code & (ROUTE_K_STRIDE - 1)
    kth = code >> ROUTE_K_SHIFT
    out = []
    for k in range(TOP_K):
        hit = kth == k
        dest = jnp.sum(jnp.where(hit, rank + vec, 0), axis=axis, keepdims=True)
        has = jnp.sum(jnp.where(hit, 1, 0), axis=axis, keepdims=True)
        out.append(jnp.where(has > 0, dest, -1))
    return out


def _scatter_kernel(nslot_ref, dst_ref, srow_ref, a_ref, cnt_ref, h2_ref, ws_ref, codet_ref, vect_ref, xw_in_ref,
                    xw_ref, stage_ref, carry_ref, sem_ref):
    del xw_in_ref
    i = pl.program_id(0)
    tm = h2_ref.shape[0]

    @pl.when(i == 0)
    def _():
        carry_ref[...] = jnp.zeros_like(carry_ref)

    rows = lax.broadcasted_iota(I32, (MOE_STAGE_ROWS, tm), 0)
    p = jnp.zeros((MOE_STAGE_ROWS, tm), F32)
    for dest in _stage_dest(codet_ref[...], vect_ref[...], 0):
        p = jnp.where(rows == dest, 1.0, p)
    p = p.astype(BF16)
    stage_ref[:, :D_MODEL] = _bdot(p, h2_ref[...])
    stage_ref[:, D_MODEL:] = _bdot(p, ws_ref[...])

    def fix(e, _):
        cnt = cnt_ref[i * N_EXPERTS + e]

        @pl.when(cnt > 0)
        def _():
            r0 = pl.multiple_of(srow_ref[i * N_EXPERTS + e], 8)
            end = a_ref[i * N_EXPERTS + e] + cnt
            stage_ref[pl.ds(r0, 8), :] = stage_ref[pl.ds(r0, 8), :] + carry_ref[e]
            q = pl.multiple_of(jnp.minimum(r0 + end // 8 * 8, MOE_STAGE_ROWS - 8), 8)
            carry_ref[e] = jnp.where(end % 8 > 0, stage_ref[pl.ds(q, 8), :], 0.0)

        return 0

    lax.fori_loop(0, N_EXPERTS, fix, 0)

    def copy(s):
        src = stage_ref.at[pl.ds(pl.multiple_of(s * MOE_CHUNK, MOE_CHUNK), MOE_CHUNK), :]
        dst = xw_ref.at[pl.ds(pl.multiple_of(dst_ref[i * MOE_SLOTS + s], 8), MOE_CHUNK), :]
        return pltpu.make_async_copy(src, dst, sem_ref.at[s])

    def start(s, _):
        copy(s).start()
        return 0

    def wait(s, _):
        copy(s).wait()
        return 0

    lax.fori_loop(0, nslot_ref[i], start, 0)
    lax.fori_loop(0, nslot_ref[i], wait, 0)


def _scatter(plan, h2, ws, codet, xw_zero):
    npad = h2.shape[0]
    tm = TOKEN_TILE
    tile = lambda w: pl.BlockSpec((tm, w), lambda i, *_: (i, 0))
    return pl.pallas_call(
        _scatter_kernel,
        grid_spec=pltpu.PrefetchScalarGridSpec(
            num_scalar_prefetch=5, grid=(npad // tm,),
            in_specs=[tile(D_MODEL), tile(LANES),
                      pl.BlockSpec((N_EXPERTS, tm), lambda i, *_: (0, i)),
                      pl.BlockSpec((None, N_EXPERTS, 1), lambda i, *_: (i, 0, 0)),
                      pl.BlockSpec(memory_space=pl.ANY)],
            out_specs=pl.BlockSpec(memory_space=pl.ANY),
            scratch_shapes=[pltpu.VMEM((MOE_STAGE_ROWS, XW_COLS), F32),
                            pltpu.VMEM((N_EXPERTS, 8, XW_COLS), F32),
                            pltpu.SemaphoreType.DMA((MOE_SLOTS,))]),
        out_shape=jax.ShapeDtypeStruct(xw_zero.shape, F32),
        input_output_aliases={9: 0},
        compiler_params=_cparams(("arbitrary",)),
        name="moe_scatter",
    )(plan['nslot'], plan['dst'], plan['srow'], plan['a'], plan['cnt'], h2, ws, codet, plan['vect'], xw_zero)


def _expert_kernel(be_ref, nu_ref, xw_ref, w1_ref, b1_ref, w2_ref, b2_ref, y_ref, w1b_ref, w2b_ref):
    i = pl.program_id(0)
    e = be_ref[i]

    @pl.when(i < nu_ref[0])
    def _():
        @pl.when((i == 0) | (be_ref[jnp.maximum(i - 1, 0)] != e))
        def _():
            w1b_ref[...] = w1_ref[...].astype(BF16)
            w2b_ref[...] = w2_ref[...].astype(BF16)

        xw = xw_ref[...]
        wl = xw[:, D_MODEL:]
        lane = lax.broadcasted_iota(I32, wl.shape, 1)
        pick = ((lane & (N_EXPERTS - 1)) == e) & (lane < 3 * N_EXPERTS)
        wrow = jnp.sum(jnp.where(pick, wl, 0.0), axis=1, keepdims=True)
        u = _bdot(xw[:, :D_MODEL].astype(BF16), w1b_ref[...]) + b1_ref[...]
        g = jnp.minimum(u[:, :D_FF], SWIGLU_LIMIT)
        lin = jnp.clip(u[:, D_FF:], -SWIGLU_LIMIT, SWIGLU_LIMIT)
        act = g * jax.nn.sigmoid(SWIGLU_ALPHA * g) * (lin + 1.0)
        y = _bdot(act.astype(BF16), w2b_ref[...]) + b2_ref[...]
        y_ref[...] = y * wrow

    @pl.when(i >= nu_ref[0])
    def _():
        y_ref[...] = jnp.zeros_like(y_ref)


def _experts(block_e, n_used, xw, w1, b1, w2, b2):
    rmax = xw.shape[0]
    nblk = rmax // MOE_BLOCK

    def row_map(i, be, nu):
        return (jnp.minimum(i, nu[0] - 1), 0)

    return pl.pallas_call(
        _expert_kernel,
        grid_spec=pltpu.PrefetchScalarGridSpec(
            num_scalar_prefetch=2, grid=(nblk,),
            in_specs=[pl.BlockSpec((MOE_BLOCK, XW_COLS), row_map),
                      pl.BlockSpec((None, D_MODEL, 2 * D_FF), lambda i, be, nu: (be[i], 0, 0)),
                      pl.BlockSpec((None, 1, 2 * D_FF), lambda i, be, nu: (be[i], 0, 0)),
                      pl.BlockSpec((None, D_FF, D_MODEL), lambda i, be, nu: (be[i], 0, 0)),
                      pl.BlockSpec((None, 1, D_MODEL), lambda i, be, nu: (be[i], 0, 0))],
            out_specs=pl.BlockSpec((MOE_BLOCK, D_MODEL), lambda i, be, nu: (i, 0)),
            scratch_shapes=[pltpu.VMEM((D_MODEL, 2 * D_FF), BF16), pltpu.VMEM((D_FF, D_MODEL), BF16)]),
        out_shape=jax.ShapeDtypeStruct((rmax, D_MODEL), F32),
        compiler_params=_cparams(("arbitrary",)),
        name="moe_experts",
    )(block_e, n_used, xw, w1, b1, w2, b2)


def _combine_kernel(nslot_ref, dst_ref, code_ref, vec_ref, xmid_ref, gf_ref, y_ref, o_ref, ybuf_ref, sem_ref):
    i = pl.program_id(0)
    tm = code_ref.shape[0]
    nslot = nslot_ref[i]

    def copy(s):
        src = y_ref.at[pl.ds(pl.multiple_of(dst_ref[i * MOE_SLOTS + s], 8), MOE_CHUNK), :]
        dst = ybuf_ref.at[pl.ds(pl.multiple_of(s * MOE_CHUNK, MOE_CHUNK), MOE_CHUNK), :]
        return pltpu.make_async_copy(src, dst, sem_ref.at[s])

    def start(s, _):
        copy(s).start()
        return 0

    def clear(s, _):
        ybuf_ref[pl.ds(pl.multiple_of(s * MOE_CHUNK, MOE_CHUNK), MOE_CHUNK), :] = jnp.zeros((MOE_CHUNK, D_MODEL), F32)
        return 0

    def wait(s, _):
        copy(s).wait()
        return 0

    lax.fori_loop(0, nslot, start, 0)
    lax.fori_loop(nslot, MOE_SLOTS, clear, 0)
    cols = lax.broadcasted_iota(I32, (tm, MOE_STAGE_ROWS), 1)
    pt = jnp.zeros((tm, MOE_STAGE_ROWS), F32)
    for dest in _stage_dest(code_ref[...], vec_ref[...], 1):
        pt = jnp.where(cols == dest, 1.0, pt)
    lax.fori_loop(0, nslot, wait, 0)
    acc = _bdot(pt.astype(BF16), ybuf_ref[...].astype(BF16))
    o_ref[...] = xmid_ref[...] + gf_ref[...] * acc


def _combine(plan, code, xmid, gf, ypad):
    npad = xmid.shape[0]
    tm = TOKEN_TILE
    tile = lambda w: pl.BlockSpec((tm, w), lambda i, *_: (i, 0))
    return pl.pallas_call(
        _combine_kernel,
        grid_spec=pltpu.PrefetchScalarGridSpec(
            num_scalar_prefetch=2, grid=(npad // tm,),
            in_specs=[tile(LANES),
                      pl.BlockSpec((None, 1, LANES), lambda i, *_: (i, 0, 0)),
                      tile(D_MODEL), tile(D_MODEL),
                      pl.BlockSpec(memory_space=pl.ANY)],
            out_specs=tile(D_MODEL),
            scratch_shapes=[pltpu.VMEM((MOE_STAGE_ROWS, D_MODEL), F32),
                            pltpu.SemaphoreType.DMA((MOE_SLOTS,))]),
        out_shape=jax.ShapeDtypeStruct((npad, D_MODEL), F32),
        compiler_params=_cparams(("arbitrary",)),
        name="moe_combine",
    )(plan['nslot'], plan['dst'], code, plan['vec'], xmid, gf, ypad)


def _moe_plan(base, total):
    ntiles = base.shape[0]
    base = base[:, 0, :N_EXPERTS]
    counts = total[0, :N_EXPERTS]
    cap = (counts + MOE_CHUNK + MOE_BLOCK - 1) // MOE_BLOCK * MOE_BLOCK
    pend = jnp.cumsum(cap)
    pstart = pend - cap
    cnt = jnp.concatenate([base[1:], counts[None]], axis=0) - base
    off = pstart[None, :] + base
    a = off % 8
    nch = jnp.where(cnt > 0, (a + cnt + MOE_CHUNK - 1) // MOE_CHUNK, 0)
    cum = jnp.cumsum(nch, axis=1)
    slot0 = cum - nch
    srow = slot0 * MOE_CHUNK
    vec = srow + a
    s = jnp.arange(MOE_SLOTS)
    e_of_s = jnp.minimum(jnp.sum(s[None, :, None] >= cum[:, None, :], axis=2), N_EXPERTS - 1)
    c_of_s = s[None, :] - jnp.take_along_axis(slot0, e_of_s, axis=1)
    dst = jnp.take_along_axis(off - a, e_of_s, axis=1) + c_of_s * MOE_CHUNK
    dst = jnp.where(s[None, :] < cum[:, -1:], dst, 0)
    rmax = (ntiles * TOKEN_TILE * TOP_K + N_EXPERTS * (MOE_CHUNK + MOE_BLOCK)) // MOE_BLOCK * MOE_BLOCK
    n_used = pend[-1] // MOE_BLOCK
    blk = jnp.arange(rmax // MOE_BLOCK)
    row0 = jnp.minimum(blk, n_used - 1) * MOE_BLOCK
    block_e = jnp.minimum(jnp.sum(row0[:, None] >= pend[None, :], axis=1), N_EXPERTS - 1)
    flat = lambda v: v.reshape(-1).astype(I32)
    pad_lanes = jnp.zeros((ntiles, LANES - N_EXPERTS), I32)
    return dict(nslot=flat(cum[:, -1]), dst=flat(dst), srow=flat(srow), a=flat(a), cnt=flat(cnt),
                vect=vec.astype(I32)[:, :, None],
                vec=jnp.concatenate([vec.astype(I32), pad_lanes], axis=1)[:, None, :],
                block_e=block_e.astype(I32), n_used=n_used.reshape(1).astype(I32), rmax=rmax)


def _final_kernel(x_ref, g_ref, o_ref):
    o_ref[...] = _rms(x_ref[...], g_ref[...])


def _final_norm(xall, g):
    npad = xall.shape[0]
    tm = TOKEN_TILE
    return pl.pallas_call(
        _final_kernel,
        grid=(npad // tm,),
        in_specs=[pl.BlockSpec((tm, D_MODEL), lambda i: (i, 0)), pl.BlockSpec((1, D_MODEL), lambda i: (0, 0))],
        out_specs=pl.BlockSpec((tm, D_MODEL), lambda i: (i, 0)),
        out_shape=jax.ShapeDtypeStruct((npad, D_MODEL), F32),
        compiler_params=_cparams(("parallel",)),
        name="final_norm",
    )(xall, g)


def _rot_cols(w):
    half = w.shape[-1] // 2
    return jnp.concatenate([-w[..., half:], w[..., :half]], axis=-1)


def _layer_weights(l, w_in, mla_w_uq, mla_w_uk, mla_w_uv, w_branch, w_out, router_w, router_b):
    idx = [0]
    for s in W_IN_SIZES:
        idx.append(idx[-1] + s)
    w = w_in[l]
    kr = w[:, idx[2]:idx[3]]
    seg0 = jnp.concatenate([w[:, :idx[3]], _rot_cols(kr), jnp.zeros((D_MODEL, 64), F32)], axis=1)
    w_ext = jnp.concatenate([seg0, w[:, idx[3]:]], axis=1).astype(BF16)
    uq = mla_w_uq[l].reshape(MLA_Q_LORA, MLA_HEADS, MLA_NOPE + MLA_ROPE)
    nope = uq[:, :, :MLA_NOPE].reshape(MLA_Q_LORA, -1)
    pe = uq[:, :, MLA_NOPE:]
    wuq = jnp.concatenate([nope, pe.reshape(MLA_Q_LORA, -1), _rot_cols(pe).reshape(MLA_Q_LORA, -1)], axis=1).astype(BF16)
    wuk = jnp.transpose(mla_w_uk[l], (1, 2, 0)).astype(BF16)
    wuv = jnp.transpose(mla_w_uv[l], (1, 0, 2))
    z = jnp.zeros_like(wuv[0])
    wuv_bd = jnp.stack([jnp.concatenate([jnp.concatenate([wuv[2 * p], z], axis=1),
                                         jnp.concatenate([z, wuv[2 * p + 1]], axis=1)], axis=0)
                        for p in range(MLA_HEADS // 2)]).astype(BF16)
    wuv_all = mla_w_uv[l].reshape(MLA_KV_LORA, BRANCH_W).astype(BF16)
    rw = jnp.concatenate([router_w[l], jnp.zeros((D_MODEL, LANES - N_EXPERTS), F32)], axis=1).astype(BF16)
    rb = jnp.concatenate([router_b[l], jnp.full((LANES - N_EXPERTS,), NEG, F32)])[None]
    return dict(w_ext=w_ext, wuq=wuq, wuk=wuk, wuv_bd=wuv_bd, wuv_all=wuv_all,
                wbr=w_branch[l].astype(BF16), wout=w_out[l].astype(BF16), rw=rw, rb=rb)


def _rope_tables(pos):
    half = MLA_ROPE // 2
    inv = ROPE_THETA ** (-jnp.arange(half, dtype=F32) / half)
    ang = pos.astype(F32)[:, None] * inv[None, :]
    cos = jnp.tile(jnp.cos(ang), (1, 2 * MLA_HEADS))
    sin = jnp.tile(jnp.sin(ang), (1, 2 * MLA_HEADS))
    return cos, sin


def _pad_rows(a, rows):
    return jnp.concatenate([a, jnp.zeros((rows - a.shape[0],) + a.shape[1:], a.dtype)], axis=0)


def kernel(x_prompt, x_sample, c_prompt, c_sample, cache_mla_ckv, cache_mla_krope, cache_sb_k, cache_sb_v, state_hgrn, page_table, ada_w, ada_b, norm_mix, norm_ffn, w_in, mla_q_norm, mla_kv_norm, mla_w_uq, mla_w_uk, mla_w_uv, hg_lower_bounds, hg_norm, w_branch, w_out, router_w, router_b, exp_w1, exp_b1, exp_w2, exp_b2, final_norm):
    bp, tp, d = x_prompt.shape
    bs, ts, _ = x_sample.shape
    depth = w_in.shape[0]
    tm = TOKEN_TILE
    n_p, n_s = bp * tp, bs * ts
    assert d == D_MODEL and tp % tm == 0 and n_s <= tm and tp % MLA_TK == 0
    npg = page_table.shape[1]
    assert npg % MLA_PAGES == 0 and npg % SB_PAGES == 0 and cache_mla_ckv.shape[2] == PAGE_SIZE
    past_len = npg * PAGE_SIZE
    nt_p = tp // tm
    tile_s = n_p // tm
    npad = n_p + tm
    ts_pad = HG_CHUNK

    xall = jnp.concatenate([x_prompt.reshape(n_p, d), _pad_rows(x_sample.reshape(n_s, d), tm)], axis=0)
    mod = _adaln(jnp.concatenate([c_prompt, c_sample], axis=0), ada_w, ada_b)
    lower = _lower_bound(hg_lower_bounds)
    cos_p, sin_p = _rope_tables(jnp.arange(tp))
    cos_s, sin_s = _rope_tables(past_len + jnp.arange(ts))
    cos_s = _pad_rows(jnp.tile(cos_s, (bs, 1)), tm)
    sin_s = _pad_rows(jnp.tile(sin_s, (bs, 1)), tm)

    r = jnp.arange(SB_T)
    u_sb = (r[:, None] >= r[None, :]).astype(BF16)
    u_page = u_sb[:PAGE_SIZE, :PAGE_SIZE]
    head_mask = (jnp.arange(BRANCH_W)[None, :] // SB_DIM == jnp.arange(SB_HEADS)[:, None]).astype(F32)
    cache_k = jnp.transpose(cache_sb_k, (0, 1, 3, 4, 2)).reshape(cache_sb_k.shape[:2] + (BRANCH_W, PAGE_SIZE))
    cache_v = jnp.transpose(cache_sb_v, (0, 1, 3, 4, 2)).reshape(cache_sb_v.shape[:2] + (BRANCH_W, PAGE_SIZE))
    cache_krt = jnp.transpose(cache_mla_krope, (0, 1, 3, 2))
    zero_state = jnp.zeros((bp, HG_HEADS, HG_DK, HG_DV), F32)

    def prompt_mod(v):
        return v[:bp, None, :]

    def sample_mod(v):
        return _pad_rows(jnp.repeat(v[bp:], ts, axis=0), tm)[None]

    def all_rows(v):
        return jnp.concatenate([jnp.repeat(v[:bp], tp, axis=0), sample_mod(v)[0]], axis=0)

    outs = [[] for _ in range(10)]
    for l in range(depth):
        lw = _layer_weights(l, w_in, mla_w_uq, mla_w_uk, mla_w_uv, w_branch, w_out, router_w, router_b)
        sh_m, sc_m, g_m, sh_f, sc_f, g_f = jnp.split(mod[l], 6, axis=-1)
        gn = norm_mix[l][None]
        qn = mla_q_norm[l][None]
        kvn = mla_kv_norm[l][None]
        lb = lower[l][None]

        (qcat, kcat, ckv, kr, sq, skb, svb, sk, sv, hq, hk, hv, hg, hgate, sig) = _mixer_in(
            xall, 0, bp, nt_p, gn, prompt_mod(sh_m), prompt_mod(sc_m), lw['w_ext'], qn, kvn,
            lw['wuq'], lw['wuk'], cos_p, sin_p, lb)
        o_mla = _mla_prompt(qcat, kcat, lw['wuv_bd'], bp, tp)
        o_sb = _sb_prompt(sq, skb, svb, u_sb, bp, tp)
        o_hg, st_p = _hgrn(hq, hk, hv, hg, zero_state, bp, tp)
        for lst, val in zip(outs[:5], (ckv.reshape(bp, tp, -1), kr.reshape(bp, tp, -1),
                                       sk.reshape(bp, tp, SB_HEADS, SB_DIM), sv.reshape(bp, tp, SB_HEADS, SB_DIM),
                                       st_p)):
            lst.append(val)

        (qcat_s, kcat_s, ckv_s, kr_s, sq_s, skb_s, svb_s, sk_s, sv_s, hq_s, hk_s, hv_s, hg_s, hgate_s, sig_s) = _mixer_in(
            xall, tile_s, 1, 1, gn, sample_mod(sh_m), sample_mod(sc_m), lw['w_ext'], qn, kvn,
            lw['wuq'], lw['wuk'], cos_s, sin_s, lb)
        q_s = jnp.transpose(qcat_s[:, :n_s].reshape(MLA_HEADS, bs, ts, MLA_CAT), (1, 2, 0, 3)).reshape(bs, ts * MLA_HEADS, MLA_CAT)
        kn_s = jnp.concatenate([kcat_s[:n_s].reshape(bs, ts, MLA_CAT),
                                jnp.zeros((bs, 16 - ts, MLA_CAT), BF16)], axis=1)
        o_mla_s = _mla_sample(l, page_table, q_s, kn_s, cache_mla_ckv, cache_krt, lw['wuv_all'], head_mask)
        q_bd = (sq_s[:n_s].reshape(bs, ts, 1, BRANCH_W) * head_mask.astype(BF16)[None, None]).reshape(bs, ts * SB_HEADS, BRANCH_W)
        pad_new = lambda a: jnp.concatenate([a[:n_s].reshape(bs, ts, BRANCH_W),
                                             jnp.zeros((bs, PAGE_SIZE - ts, BRANCH_W), BF16)], axis=1)
        o_sb_s = _sb_sample(l, page_table, q_bd, pad_new(skb_s), pad_new(svb_s), cache_k, cache_v, u_page, head_mask)
        pad_t = lambda a: jnp.concatenate([a[:n_s].reshape(bs, ts, 512),
                                           jnp.zeros((bs, ts_pad - ts, 512), F32)], axis=1).reshape(bs * ts_pad, 512)
        o_hg_s, st_s = _hgrn(pad_t(hq_s), pad_t(hk_s), pad_t(hv_s), pad_t(hg_s), state_hgrn[l], bs, ts_pad)
        o_hg_s = o_hg_s.reshape(bs, ts_pad, 512)[:, :ts].reshape(n_s, 512)
        out_weights = (hg_norm[l][None], lw['wbr'], lw['wout'], norm_ffn[l][None], lw['rw'], lw['rb'])
        tail = _mixer_out(
            xall, tile_s, 1, 1, _pad_rows(o_mla_s.reshape(n_s, 512), tm), _pad_rows(o_sb_s.reshape(n_s, 512), tm),
            _pad_rows(o_hg_s, tm), hgate_s, sig_s, sample_mod(g_m), sample_mod(sh_f), sample_mod(sc_f),
            *out_weights, None)
        xmid, h2, logits = _mixer_out(xall, 0, bp, nt_p, o_mla, o_sb, o_hg, hgate, sig,
                                      prompt_mod(g_m), prompt_mod(sh_f), prompt_mod(sc_f), *out_weights, tail)
        for lst, val in zip(outs[5:], (ckv_s[:n_s].reshape(bs, ts, -1), kr_s[:n_s].reshape(bs, ts, -1),
                                       sk_s[:n_s].reshape(bs, ts, SB_HEADS, SB_DIM),
                                       sv_s[:n_s].reshape(bs, ts, SB_HEADS, SB_DIM), st_s)):
            lst.append(val)

        ws, code, codet, base, total = _route(logits, n_p + n_s)
        plan = _moe_plan(base, total)
        xw = _scatter(plan, h2, ws, codet, jnp.zeros((plan['rmax'], XW_COLS), F32))
        ypad = _experts(plan['block_e'], plan['n_used'], xw, exp_w1[l], exp_b1[l][:, None, :],
                        exp_w2[l], exp_b2[l][:, None, :])
        xall = _combine(plan, code, xmid, all_rows(g_f), ypad)

    y = _final_norm(xall, final_norm[None])
    y_prompt = y[:n_p].reshape(bp, tp, d)
    y_sample = y[n_p:n_p + n_s].reshape(bs, ts, d)
    st = [jnp.stack(o) for o in outs]
    return (y_prompt, y_sample, st[0], st[1], st[2], st[3], st[4], st[5], st[6], st[7], st[8], st[9])
```

```python
import functools

import jax
import jax.numpy as jnp
from jax import lax
from jax.experimental import pallas as pl
from jax.experimental.pallas import tpu as pltpu

F32 = jnp.float32
BF16 = jnp.bfloat16
I32 = jnp.int32

D_MODEL = 1024
MLA_HEADS, MLA_Q_LORA, MLA_KV_LORA, MLA_NOPE, MLA_ROPE, MLA_V = 8, 256, 128, 64, 32, 64
MLA_CAT = MLA_KV_LORA + MLA_ROPE
ROPE_THETA = 10000.0
MLA_SCALE = (MLA_NOPE + MLA_ROPE) ** -0.5
SB_HEADS, SB_DIM = 8, 64
SB_SCALE = SB_DIM ** -0.5
HG_HEADS, HG_DK, HG_DV = 4, 128, 128
BRANCH_W = 512
N_BRANCH = 3
N_EXPERTS, TOP_K, D_FF = 32, 4, 1024
SWIGLU_LIMIT, SWIGLU_ALPHA = 7.0, 1.702
PAGE_SIZE = 128
EPS = 1e-6

V7X_VMEM_BYTES = 64 * 1024 * 1024
VMEM_LIMIT = V7X_VMEM_BYTES * 7 // 8
LANES = 128
TOKEN_TILE = 256
MLA_TQ, MLA_TK = 128, 256
SB_T = 256
SB_PROMPT_HEADS = 4
HG_CHUNK = 16
MLA_PAGES = 32
SB_PAGES = 16
MOE_CHUNK = 16
MOE_BLOCK = 256
MOE_SLOTS = -(-(N_EXPERTS + (TOKEN_TILE * TOP_K + N_EXPERTS * 7) // MOE_CHUNK + 1) // 8) * 8
MOE_STAGE_ROWS = MOE_SLOTS * MOE_CHUNK
ROUTE_K_SHIFT = 10
ROUTE_K_STRIDE = 1 << ROUTE_K_SHIFT
XW_COLS = D_MODEL + LANES
NEG = -1e30

W_IN_SIZES = (MLA_Q_LORA, MLA_KV_LORA, MLA_ROPE, BRANCH_W, BRANCH_W, BRANCH_W,
              HG_HEADS * HG_DK, HG_HEADS * HG_DK, BRANCH_W, BRANCH_W, N_BRANCH * D_MODEL)
W_EXT_HEAD = 512
CKV_COL = MLA_Q_LORA
KR_COL = CKV_COL + MLA_KV_LORA
KR_ROT_COL = KR_COL + MLA_ROPE
SQ_COL, SK_COL, SV_COL, HF_COL, HQ_COL, HI_COL, HGATE_COL, GATES_COL = (
    W_EXT_HEAD + sum(W_IN_SIZES[3:3 + n]) for n in range(8))


def _cparams(sem):
    return pltpu.CompilerParams(dimension_semantics=sem, vmem_limit_bytes=VMEM_LIMIT)


def _bdot(a, b):
    return jnp.dot(a, b, preferred_element_type=F32)


def _dot_nt(a, b):
    return lax.dot_general(a, b, (((1,), (1,)), ((), ())), preferred_element_type=F32)


def _dot_tn(a, b):
    return lax.dot_general(a, b, (((0,), (0,)), ((), ())), preferred_element_type=F32)


def _rms(x, g):
    return x * lax.rsqrt(jnp.mean(x * x, axis=-1, keepdims=True) + EPS) * g


def _split2(x):
    hi = x.astype(BF16)
    lo = (x - hi.astype(F32)).astype(BF16)
    return hi, lo


def _split3(x):
    hi = x.astype(BF16)
    r = x - hi.astype(F32)
    mid = r.astype(BF16)
    lo = (r - mid.astype(F32)).astype(BF16)
    return hi, mid, lo


def _adaln_kernel(c_ref, w_ref, b_ref, o_ref):
    c = c_ref[...]
    s = (c * jax.nn.sigmoid(c)).astype(BF16)
    o_ref[...] = _bdot(s, w_ref[...].astype(BF16)) + b_ref[...]


def _adaln(c_all, ada_w, ada_b):
    depth, d, six_d = ada_w.shape
    nc = c_all.shape[0]
    return pl.pallas_call(
        _adaln_kernel,
        grid=(depth, six_d // d),
        in_specs=[pl.BlockSpec((nc, d), lambda l, j: (0, 0)),
                  pl.BlockSpec((None, d, d), lambda l, j: (l, 0, j)),
                  pl.BlockSpec((None, 1, d), lambda l, j: (l, 0, j))],
        out_specs=pl.BlockSpec((None, nc, d), lambda l, j: (l, 0, j)),
        out_shape=jax.ShapeDtypeStruct((depth, nc, six_d), F32),
        compiler_params=_cparams(("parallel", "parallel")),
        name="adaln",
    )(c_all, ada_w, ada_b.reshape(depth, 1, six_d))


def _lower_bound_kernel(x_ref, o_ref):
    x = x_ref[...]
    e = jnp.exp(x - jnp.max(x, axis=0, keepdims=True))
    sm = e / jnp.sum(e, axis=0, keepdims=True)
    acc = jnp.zeros_like(sm[0:1])
    for l in range(x.shape[0]):
        acc = acc + sm[l:l + 1]
        o_ref[l:l + 1, :] = acc - sm[0:1]


def _lower_bound(hg_lower_bounds):
    return pl.pallas_call(
        _lower_bound_kernel,
        out_shape=jax.ShapeDtypeStruct(hg_lower_bounds.shape, F32),
        name="lower_bound",
    )(hg_lower_bounds)


def _mixer_in_kernel(x_ref, gn_ref, sh_ref, sc_ref, w_ref, qn_ref, kvn_ref, wuq_ref, wuk_ref,
                     cos_ref, sin_ref, lb_ref,
                     qcat_ref, kcat_ref, ckv_ref, kr_ref, sq_ref, skb_ref, svb_ref, sk_ref, sv_ref,
                     hq_ref, hk_ref, hv_ref, hg_ref, hgate_ref, sig_ref):
    x = x_ref[...]
    h = (_rms(x, gn_ref[...]) * (1.0 + sc_ref[...]) + sh_ref[...]).astype(BF16)

    def seg(start, width):
        return _bdot(h, w_ref[:, start:start + width])

    u0 = seg(0, W_EXT_HEAD)
    cq = u0[:, :MLA_Q_LORA]
    ckv = _rms(u0[:, CKV_COL:KR_COL], kvn_ref[...])
    cos8 = cos_ref[...]
    sin8 = sin_ref[...]
    kr = (u0[:, KR_COL:KR_ROT_COL] * cos8[:, :MLA_ROPE]
          + u0[:, KR_ROT_COL:KR_ROT_COL + MLA_ROPE] * sin8[:, :MLA_ROPE])
    ckv_ref[...] = ckv
    kr_ref[...] = kr
    kcat_ref[:, :MLA_KV_LORA] = ckv.astype(BF16)
    kcat_ref[:, MLA_KV_LORA:] = kr.astype(BF16)

    qq = _bdot(_rms(cq, qn_ref[...]).astype(BF16), wuq_ref[...])
    n_nope, n_pe = MLA_HEADS * MLA_NOPE, MLA_HEADS * MLA_ROPE
    pe = qq[:, n_nope:n_nope + n_pe] * cos8 + qq[:, n_nope + n_pe:] * sin8
    for hd in range(MLA_HEADS):
        qn = qq[:, hd * MLA_NOPE:(hd + 1) * MLA_NOPE].astype(BF16)
        qcat_ref[hd, :, :MLA_KV_LORA] = _bdot(qn, wuk_ref[hd]).astype(BF16)
        qcat_ref[hd, :, MLA_KV_LORA:] = pe[:, hd * MLA_ROPE:(hd + 1) * MLA_ROPE].astype(BF16)

    sq_ref[...] = (seg(SQ_COL, BRANCH_W) * SB_SCALE).astype(BF16)
    sk = seg(SK_COL, BRANCH_W)
    sk_ref[...] = sk
    skb_ref[...] = sk.astype(BF16)
    sv = seg(SV_COL, BRANCH_W)
    sv_ref[...] = sv
    svb_ref[...] = sv.astype(BF16)
    lb = lb_ref[...]
    f = lb + (1.0 - lb) * jax.nn.sigmoid(seg(HF_COL, HG_HEADS * HG_DK))
    hg_ref[...] = jnp.log(f)
    hk_ref[...] = 1.0 - f
    hq_ref[...] = seg(HQ_COL, HG_HEADS * HG_DK)
    hv_ref[...] = seg(HI_COL, BRANCH_W)
    hgate_ref[...] = seg(HGATE_COL, BRANCH_W)
    for n in range(N_BRANCH):
        c0 = GATES_COL + n * D_MODEL
        sig_ref[:, n * D_MODEL:(n + 1) * D_MODEL] = jax.nn.sigmoid(seg(c0, D_MODEL)).astype(BF16)


def _mixer_in(xall, tile0, nb, nt, gn, sh, sc, w_ext, qn, kvn, wuq, wuk, cos8, sin8, lb):
    tm = TOKEN_TILE
    rows = nb * nt * tm
    rmod = sh.shape[1]
    row_spec = lambda w: pl.BlockSpec((tm, w), lambda b, i: (b * nt + i, 0))
    full = lambda a: pl.BlockSpec(a.shape, lambda b, i: (0,) * a.ndim)
    mod_spec = pl.BlockSpec((None, rmod, D_MODEL), lambda b, i: (b, 0, 0))
    pos_spec = pl.BlockSpec((tm, 256), lambda b, i: (i, 0))
    out_shapes = [
        jax.ShapeDtypeStruct((MLA_HEADS, rows, MLA_CAT), BF16),
        jax.ShapeDtypeStruct((rows, MLA_CAT), BF16),
        jax.ShapeDtypeStruct((rows, MLA_KV_LORA), F32),
        jax.ShapeDtypeStruct((rows, MLA_ROPE), F32),
        jax.ShapeDtypeStruct((rows, 512), BF16),
        jax.ShapeDtypeStruct((rows, 512), BF16),
        jax.ShapeDtypeStruct((rows, 512), BF16),
        jax.ShapeDtypeStruct((rows, 512), F32),
        jax.ShapeDtypeStruct((rows, 512), F32),
        jax.ShapeDtypeStruct((rows, 512), F32),
        jax.ShapeDtypeStruct((rows, 512), F32),
        jax.ShapeDtypeStruct((rows, 512), F32),
        jax.ShapeDtypeStruct((rows, 512), F32),
        jax.ShapeDtypeStruct((rows, 512), F32),
        jax.ShapeDtypeStruct((rows, N_BRANCH * D_MODEL), BF16),
    ]
    out_specs = [pl.BlockSpec((MLA_HEADS, tm, MLA_CAT), lambda b, i: (0, b * nt + i, 0))]
    out_specs += [row_spec(s.shape[1]) for s in out_shapes[1:]]
    return pl.pallas_call(
        _mixer_in_kernel,
        grid=(nb, nt),
        in_specs=[pl.BlockSpec((tm, D_MODEL), lambda b, i: (tile0 + b * nt + i, 0)),
                  full(gn), mod_spec, mod_spec, full(w_ext), full(qn), full(kvn), full(wuq), full(wuk),
                  pos_spec, pos_spec, full(lb)],
        out_specs=out_specs,
        out_shape=out_shapes,
        compiler_params=_cparams(("parallel", "parallel")),
        name="mixer_in",
    )(xall, gn, sh, sc, w_ext, qn, kvn, wuq, wuk, cos8, sin8, lb)


def _mla_prompt_kernel(q_ref, k_ref, wuv_ref, o_ref, *, tq, tk):
    i = pl.program_id(1)
    rows = MLA_HEADS * tq
    q = q_ref[...].reshape(rows, MLA_CAT)
    last = (i * tq + tq - 1) // tk

    def step(j, carry, masked):
        m, l, acc = carry
        k = k_ref[pl.ds(pl.multiple_of(j * tk, tk), tk), :]
        s = _dot_nt(q, k) * MLA_SCALE
        if masked:
            qpos = i * tq + (lax.broadcasted_iota(I32, (rows, tk), 0) & (tq - 1))
            kpos = j * tk + lax.broadcasted_iota(I32, (rows, tk), 1)
            s = jnp.where(kpos <= qpos, s, NEG)
        m_new = jnp.maximum(m, jnp.max(s, axis=1, keepdims=True))
        alpha = jnp.exp(m - m_new)
        p = jnp.exp(s - m_new)
        l = alpha * l + jnp.sum(p, axis=1, keepdims=True)
        acc = alpha * acc + _bdot(p.astype(BF16), k[:, :MLA_KV_LORA])
        return m_new, l, acc

    init = (jnp.full((rows, 1), NEG, F32), jnp.zeros((rows, 1), F32), jnp.zeros((rows, MLA_KV_LORA), F32))
    carry = lax.fori_loop(0, last, lambda j, c: step(j, c, False), init)
    _, l, acc = step(last, carry, True)
    o = (acc / l).astype(BF16)
    for hp in range(MLA_HEADS // 2):
        pair = jnp.concatenate([o[(2 * hp) * tq:(2 * hp + 1) * tq], o[(2 * hp + 1) * tq:(2 * hp + 2) * tq]], axis=1)
        o_ref[:, hp * LANES:(hp + 1) * LANES] = _bdot(pair, wuv_ref[hp]).astype(BF16)


def _mla_prompt(qcat, kcat, wuv_bd, nb, t):
    tq, tk = MLA_TQ, MLA_TK
    nq = t // tq
    return pl.pallas_call(
        functools.partial(_mla_prompt_kernel, tq=tq, tk=tk),
        grid=(nb, nq),
        in_specs=[pl.BlockSpec((MLA_HEADS, tq, MLA_CAT), lambda b, i: (0, b * nq + i, 0)),
                  pl.BlockSpec((t, MLA_CAT), lambda b, i: (b, 0)),
                  pl.BlockSpec(wuv_bd.shape, lambda b, i: (0, 0, 0))],
        out_specs=pl.BlockSpec((tq, BRANCH_W), lambda b, i: (b * nq + i, 0)),
        out_shape=jax.ShapeDtypeStruct((nb * t, BRANCH_W), BF16),
        compiler_params=_cparams(("parallel", "parallel")),
        name="mla_prompt",
    )(qcat, kcat, wuv_bd)


def _neg_softplus(z):
    return -(jnp.maximum(z, 0.0) + jnp.log(1.0 + jnp.exp(-jnp.abs(z))))


def _sb_tiles(qs, ks, vs, u, rests, valid):
    zs = [_dot_nt(q, k) for q, k in zip(qs, ks)]
    lks = [_neg_softplus(z) for z in zs]
    if valid is not None:
        lks = [jnp.where(valid, lk, 0.0) for lk in lks]
    parts = [_split2(lk) for lk in lks]
    incls = [_bdot(hi, u) + _bdot(lo, u) for hi, lo in parts]
    ws = [jnp.exp(z + incl + rest) for z, incl, rest in zip(zs, incls, rests)]
    if valid is not None:
        ws = [jnp.where(valid, w, 0.0) for w in ws]
    outs = [_bdot(w.astype(BF16), v) for w, v in zip(ws, vs)]
    return [(o, rest + incl[:, 0:1]) for o, incl, rest in zip(outs, incls, rests)]


def _sb_prompt_kernel(q_ref, k_ref, v_ref, u_ref, o_ref, *, t, nh):
    i = pl.program_id(2)
    u = u_ref[...]
    q = q_ref[...]
    row = lax.broadcasted_iota(I32, (t, t), 0)
    col = lax.broadcasted_iota(I32, (t, t), 1)
    valid = col < row
    heads = [slice(hh * SB_DIM, (hh + 1) * SB_DIM) for hh in range(nh)]
    qs = [q[:, hs] for hs in heads]

    def block(j, carry, mask):
        r0 = pl.multiple_of(j * t, t)
        k = k_ref[pl.ds(r0, t), :]
        v = v_ref[pl.ds(r0, t), :]
        res = _sb_tiles(qs, [k[:, hs] for hs in heads], [v[:, hs] for hs in heads], u,
                        [c[1] for c in carry], mask)
        return tuple((c[0] + o, rest) for c, (o, rest) in zip(carry, res))

    zero = (jnp.zeros((t, SB_DIM), F32), jnp.zeros((t, 1), F32))
    carry = block(i, (zero,) * nh, valid)
    carry = lax.fori_loop(0, i, lambda jj, c: block(i - 1 - jj, c, None), carry)
    o_ref[...] = jnp.concatenate([c[0] for c in carry], axis=1).astype(BF16)


def _sb_prompt(sq, sk, sv, u, nb, t):
    tb = SB_T
    nq = t // tb
    nh = SB_PROMPT_HEADS
    width = nh * SB_DIM
    return pl.pallas_call(
        functools.partial(_sb_prompt_kernel, t=tb, nh=nh),
        grid=(nb, SB_HEADS // nh, nq),
        in_specs=[pl.BlockSpec((tb, width), lambda b, hg, i: (b * nq + i, hg)),
                  pl.BlockSpec((t, width), lambda b, hg, i: (b, hg)),
                  pl.BlockSpec((t, width), lambda b, hg, i: (b, hg)),
                  pl.BlockSpec((tb, tb), lambda b, hg, i: (0, 0))],
        out_specs=pl.BlockSpec((tb, width), lambda b, hg, i: (b * nq + i, hg)),
        out_shape=jax.ShapeDtypeStruct((nb * t, BRANCH_W), BF16),
        compiler_params=_cparams(("parallel", "parallel", "parallel")),
        name="sb_prompt",
    )(sq, sk, sv, u)


def _hgrn_kernel(q_ref, k_ref, v_ref, g_ref, s0_ref, l_ref, o_ref, sf_ref, st_ref, bc_ref, *, tt, nt):
    i = pl.program_id(1)
    c = HG_CHUNK

    @pl.when(i == 0)
    def _():
        for hd in range(HG_HEADS):
            st_ref[hd] = s0_ref[hd].T

    lmat = l_ref[...]
    g1, g2, g3 = _split3(g_ref[...])
    bc_ref[...] = _bdot(lmat, g1) + _bdot(lmat, g2) + _bdot(lmat, g3)
    row = lax.broadcasted_iota(I32, (c, HG_DK), 0)

    def chunk(ci, _):
        r0 = pl.multiple_of(ci * c, c)
        for hd in range(HG_HEADS):
            sl = (pl.ds(r0, c), slice(hd * HG_DK, (hd + 1) * HG_DK))
            bc = bc_ref[sl]
            q = q_ref[sl]
            k = k_ref[sl]
            v = v_ref[sl]
            st = st_ref[hd]
            inter = _dot_nt((q * jnp.exp(bc)).astype(BF16), st.astype(BF16))
            intra = jnp.zeros((c, HG_DV), F32)
            for s in range(c):
                decay = jnp.where(row >= s, jnp.exp(bc - bc[s:s + 1, :]), 0.0)
                att = jnp.sum(q * (k[s:s + 1, :] * decay), axis=1, keepdims=True)
                intra = intra + att * v[s:s + 1, :]
            o_ref[sl] = inter + intra
            bend = bc[c - 1:c, :]
            kend = k * jnp.exp(bend - bc)
            st_ref[hd] = st * jnp.exp(bend) + _dot_tn(v.astype(BF16), kend.astype(BF16))
        return 0

    lax.fori_loop(0, tt // c, chunk, 0)

    @pl.when(i == nt - 1)
    def _():
        for hd in range(HG_HEADS):
            sf_ref[hd] = st_ref[hd].T


def _hgrn(hq, hk, hv, hg, s0, nb, t):
    tt = min(t, TOKEN_TILE)
    nt = t // tt
    r = jnp.arange(tt)
    lmat = ((r[:, None] >= r[None, :]) & (r[:, None] // HG_CHUNK == r[None, :] // HG_CHUNK)).astype(BF16)
    row_spec = pl.BlockSpec((tt, 512), lambda b, i: (b * nt + i, 0))
    st_spec = pl.BlockSpec((None, HG_HEADS, HG_DK, HG_DV), lambda b, i: (b, 0, 0, 0))
    return pl.pallas_call(
        functools.partial(_hgrn_kernel, tt=tt, nt=nt),
        grid=(nb, nt),
        in_specs=[row_spec, row_spec, row_spec, row_spec, st_spec,
                  pl.BlockSpec((tt, tt), lambda b, i: (0, 0))],
        out_specs=[row_spec, st_spec],
        out_shape=[jax.ShapeDtypeStruct((nb * t, 512), F32),
                   jax.ShapeDtypeStruct((nb, HG_HEADS, HG_DK, HG_DV), F32)],
        scratch_shapes=[pltpu.VMEM((HG_HEADS, HG_DV, HG_DK), F32), pltpu.VMEM((tt, 512), F32)],
        compiler_params=_cparams(("parallel", "arbitrary")),
        name="hgrn",
    )(hq, hk, hv, hg, s0, lmat)


def _diag_heads(res, mask):
    tdim = res.shape[0] // 8
    return jnp.sum(res.reshape(tdim, 8, BRANCH_W) * mask[None], axis=1)


def _page_pipeline(pt_ref, caches, bufs, sem_ref, *, layer, npg, npp, nsteps, newest_first):
    b = pl.program_id(0)
    j = pl.program_id(1)
    step = b * nsteps + j
    slot = step % 2

    def copies(bb, jj, sl):
        out = []
        for n in range(npp):
            logical = (npg - (jj + 1) * npp + n) if newest_first else (jj * npp + n)
            page = pt_ref[bb * npg + logical]
            for ci, (cache, buf) in enumerate(zip(caches, bufs)):
                out.append(pltpu.make_async_copy(cache.at[layer, page], buf.at[sl, n], sem_ref.at[ci, sl, n]))
        return out

    @pl.when(step == 0)
    def _():
        for cp in copies(b, j, slot):
            cp.start()

    @pl.when(step + 1 < pl.num_programs(0) * nsteps)
    def _():
        wrap = j + 1 == nsteps
        for cp in copies(jnp.where(wrap, b + 1, b), jnp.where(wrap, 0, j + 1), 1 - slot):
            cp.start()

    for cp in copies(b, j, slot):
        cp.wait()
    return slot


def _mla_sample_kernel(pt_ref, q_ref, kn_ref, ckv_hbm, kr_hbm, wuv_ref, mask_ref, o_ref,
                       ckv_buf, kr_buf, sem_ref, m_ref, l_ref, acc_ref, *, layer, npg, npp, nsteps, ts):
    j = pl.program_id(1)
    rows = ts * MLA_HEADS
    q = q_ref[...]
    ql = q[:, :MLA_KV_LORA]
    qp = q[:, MLA_KV_LORA:]
    slot = _page_pipeline(pt_ref, (ckv_hbm, kr_hbm), (ckv_buf, kr_buf), sem_ref, layer=layer, npg=npg,
                          npp=npp, nsteps=nsteps, newest_first=False)

    @pl.when(j == 0)
    def _():
        kn = kn_ref[...]
        nk = kn.shape[0]
        s = _dot_nt(q, kn) * MLA_SCALE
        trow = lax.broadcasted_iota(I32, (rows, nk), 0) // MLA_HEADS
        tcol = lax.broadcasted_iota(I32, (rows, nk), 1)
        s = jnp.where(tcol <= trow, s, NEG)
        m = jnp.max(s, axis=1, keepdims=True)
        p = jnp.exp(s - m)
        m_ref[...] = m
        l_ref[...] = jnp.sum(p, axis=1, keepdims=True)
        acc_ref[...] = _bdot(p.astype(BF16), kn[:, :MLA_KV_LORA])

    ck = ckv_buf[slot].reshape(npp * PAGE_SIZE, MLA_KV_LORA).astype(BF16)
    krt = jnp.concatenate([kr_buf[slot, n] for n in range(npp)], axis=1).astype(BF16)
    s = (_dot_nt(ql, ck) + _bdot(qp, krt)) * MLA_SCALE
    m_old = m_ref[...]
    m_new = jnp.maximum(m_old, jnp.max(s, axis=1, keepdims=True))
    alpha = jnp.exp(m_old - m_new)
    p = jnp.exp(s - m_new)
    m_ref[...] = m_new
    l_ref[...] = alpha * l_ref[...] + jnp.sum(p, axis=1, keepdims=True)
    acc_ref[...] = alpha * acc_ref[...] + _bdot(p.astype(BF16), ck)

    @pl.when(j == nsteps - 1)
    def _():
        o = (acc_ref[...] / l_ref[...]).astype(BF16)
        o_ref[...] = _diag_heads(_bdot(o, wuv_ref[...]), mask_ref[...]).astype(BF16)


def _mla_sample(layer, page_table, q_s, kn_s, cache_ckv, cache_kr, wuv_all, head_mask):
    bs, npg = page_table.shape
    npp = MLA_PAGES
    nsteps = npg // npp
    rows = q_s.shape[1]
    ts = rows // MLA_HEADS

    in_specs = [pl.BlockSpec((None, rows, MLA_CAT), lambda b, j, pt: (b, 0, 0)),
                pl.BlockSpec((None,) + kn_s.shape[1:], lambda b, j, pt: (b, 0, 0)),
                pl.BlockSpec(memory_space=pl.ANY), pl.BlockSpec(memory_space=pl.ANY),
                pl.BlockSpec(wuv_all.shape, lambda b, j, pt: (0, 0)),
                pl.BlockSpec(head_mask.shape, lambda b, j, pt: (0, 0))]
    return pl.pallas_call(
        functools.partial(_mla_sample_kernel, layer=layer, npg=npg, npp=npp, nsteps=nsteps, ts=ts),
        grid_spec=pltpu.PrefetchScalarGridSpec(
            num_scalar_prefetch=1, grid=(bs, nsteps), in_specs=in_specs,
            out_specs=pl.BlockSpec((None, ts, BRANCH_W), lambda b, j, pt: (b, 0, 0)),
            scratch_shapes=[pltpu.VMEM((2, npp, PAGE_SIZE, MLA_KV_LORA), F32),
                            pltpu.VMEM((2, npp, MLA_ROPE, PAGE_SIZE), F32),
                            pltpu.SemaphoreType.DMA((2, 2, npp)),
                            pltpu.VMEM((rows, 1), F32), pltpu.VMEM((rows, 1), F32),
                            pltpu.VMEM((rows, MLA_KV_LORA), F32)]),
        out_shape=jax.ShapeDtypeStruct((bs, ts, BRANCH_W), BF16),
        compiler_params=_cparams(("arbitrary", "arbitrary")),
        name="mla_sample",
    )(page_table.reshape(-1), q_s, kn_s, cache_ckv, cache_kr, wuv_all, head_mask)


def _sb_sample_kernel(pt_ref, q_ref, kn_ref, vn_ref, k_hbm, v_hbm, u_ref, mask_ref, o_ref,
                      k_buf, v_buf, sem_ref, rest_ref, acc_ref, *, layer, npg, npp, nsteps, ts):
    j = pl.program_id(1)
    rows = ts * SB_HEADS
    q = q_ref[...]
    u = u_ref[...]
    slot = _page_pipeline(pt_ref, (k_hbm, v_hbm), (k_buf, v_buf), sem_ref, layer=layer, npg=npg,
                          npp=npp, nsteps=nsteps, newest_first=True)

    @pl.when(j == 0)
    def _():
        nk = kn_ref.shape[0]
        trow = lax.broadcasted_iota(I32, (rows, nk), 0) // SB_HEADS
        tcol = lax.broadcasted_iota(I32, (rows, nk), 1)
        (o, r), = _sb_tiles([q], [kn_ref[...]], [vn_ref[...]], u, [jnp.zeros((rows, 1), F32)], tcol < trow)
        acc_ref[...] = o
        rest_ref[...] = r

    kt = jnp.concatenate([k_buf[slot, n] for n in range(npp)], axis=1).astype(BF16)
    vt = jnp.concatenate([v_buf[slot, n] for n in range(npp)], axis=1).astype(BF16)
    z = _bdot(q, kt)
    hi, lo = _split2(_neg_softplus(z))
    pages = [slice(n * PAGE_SIZE, (n + 1) * PAGE_SIZE) for n in range(npp)]
    stacked = jnp.concatenate([hi[:, p] for p in pages] + [lo[:, p] for p in pages], axis=0)
    cs = _bdot(stacked, u)
    incl = [cs[n * rows:(n + 1) * rows] + cs[(npp + n) * rows:(npp + n + 1) * rows] for n in range(npp)]
    r = rest_ref[...]
    shifted = [None] * npp
    for n in reversed(range(npp)):
        shifted[n] = incl[n] + r
        r = r + incl[n][:, 0:1]
    rest_ref[...] = r
    w = jnp.exp(z + jnp.concatenate(shifted, axis=1))
    acc_ref[...] = acc_ref[...] + _dot_nt(w.astype(BF16), vt)

    @pl.when(j == nsteps - 1)
    def _():
        o_ref[...] = _diag_heads(acc_ref[...], mask_ref[...]).astype(BF16)


def _sb_sample(layer, page_table, q_bd, kn, vn, cache_k, cache_v, u, head_mask):
    bs, npg = page_table.shape
    npp = SB_PAGES
    nsteps = npg // npp
    rows = q_bd.shape[1]
    ts = rows // SB_HEADS
    page_shape = (2, npp, BRANCH_W, PAGE_SIZE)

    new_spec = pl.BlockSpec((None, PAGE_SIZE, BRANCH_W), lambda b, j, pt: (b, 0, 0))
    in_specs = [pl.BlockSpec((None, rows, BRANCH_W), lambda b, j, pt: (b, 0, 0)), new_spec, new_spec,
                pl.BlockSpec(memory_space=pl.ANY), pl.BlockSpec(memory_space=pl.ANY),
                pl.BlockSpec(u.shape, lambda b, j, pt: (0, 0)),
                pl.BlockSpec(head_mask.shape, lambda b, j, pt: (0, 0))]
    return pl.pallas_call(
        functools.partial(_sb_sample_kernel, layer=layer, npg=npg, npp=npp, nsteps=nsteps, ts=ts),
        grid_spec=pltpu.PrefetchScalarGridSpec(
            num_scalar_prefetch=1, grid=(bs, nsteps), in_specs=in_specs,
            out_specs=pl.BlockSpec((None, ts, BRANCH_W), lambda b, j, pt: (b, 0, 0)),
            scratch_shapes=[pltpu.VMEM(page_shape, F32), pltpu.VMEM(page_shape, F32),
                            pltpu.SemaphoreType.DMA((2, 2, npp)),
                            pltpu.VMEM((rows, 1), F32), pltpu.VMEM((rows, BRANCH_W), F32)]),
        out_shape=jax.ShapeDtypeStruct((bs, ts, BRANCH_W), BF16),
        compiler_params=_cparams(("arbitrary", "arbitrary")),
        name="sb_sample",
    )(page_table.reshape(-1), q_bd, kn, vn, cache_k, cache_v, u, head_mask)


def _mixer_out_kernel(*refs, n_compute, has_tail):
    n_in = 15
    outs = refs[n_in + (3 if has_tail else 0):]
    if not has_tail:
        _mixer_out_tile(*refs[:n_in], *outs)
        return
    i = pl.program_id(0)

    @pl.when(i < n_compute)
    def _():
        _mixer_out_tile(*refs[:n_in], *outs)

    @pl.when(i == n_compute)
    def _():
        for src, dst in zip(refs[n_in:n_in + 3], outs):
            dst[...] = src[...]


def _mixer_out_tile(x_ref, omla_ref, osb_ref, ohg_ref, hgate_ref, sig_ref, gm_ref, shf_ref, scf_ref,
                    hgn_ref, wbr_ref, wout_ref, nf_ref, rw_ref, rb_ref,
                    xmid_ref, h2_ref, logit_ref):
    ohg = ohg_ref[...]
    hgn = hgn_ref[...]
    normed = jnp.concatenate(
        [_rms(ohg[:, hd * HG_DV:(hd + 1) * HG_DV], hgn) for hd in range(HG_HEADS)], axis=1)
    gate = hgate_ref[...]
    br_hg = (normed * (gate * jax.nn.sigmoid(gate))).astype(BF16)
    branches = (omla_ref[...], osb_ref[...], br_hg)
    merged = jnp.zeros((x_ref.shape[0], D_MODEL), F32)
    for n in range(N_BRANCH):
        sig = sig_ref[:, n * D_MODEL:(n + 1) * D_MODEL].astype(F32)
        merged = merged + sig * _bdot(branches[n], wbr_ref[n])
    x = x_ref[...] + gm_ref[...] * _bdot(merged.astype(BF16), wout_ref[...])
    xmid_ref[...] = x
    h2 = (_rms(x, nf_ref[...]) * (1.0 + scf_ref[...]) + shf_ref[...]).astype(BF16)
    h2_ref[...] = h2
    logit_ref[...] = _bdot(h2, rw_ref[...]) + rb_ref[...]


def _mixer_out(xall, tile0, nb, nt, omla, osb, ohg, hgate, sig, gm, shf, scf, hgn, wbr, wout, nf, rw, rb, tail):
    tm = TOKEN_TILE
    rmod = gm.shape[1]
    n_compute = nb * nt
    ntile = n_compute + (0 if tail is None else 1)
    last = n_compute - 1
    glob = lambda w: pl.BlockSpec((tm, w), lambda i: (tile0 + jnp.minimum(i, last), 0))
    loc = lambda w: pl.BlockSpec((tm, w), lambda i: (jnp.minimum(i, last), 0))
    out = lambda w: pl.BlockSpec((tm, w), lambda i: (i, 0))
    full = lambda a: pl.BlockSpec(a.shape, lambda i: (0,) * a.ndim)
    mod_spec = pl.BlockSpec((None, rmod, D_MODEL), lambda i: (jnp.minimum(i, last) // nt, 0, 0))
    in_specs = [glob(D_MODEL), loc(512), loc(512), loc(512), loc(512), loc(N_BRANCH * D_MODEL),
                mod_spec, mod_spec, mod_spec, full(hgn), full(wbr), full(wout), full(nf), full(rw), full(rb)]
    args = [xall, omla, osb, ohg, hgate, sig, gm, shf, scf, hgn, wbr, wout, nf, rw, rb]
    if tail is not None:
        in_specs += [full(t) for t in tail]
        args += list(tail)
    return pl.pallas_call(
        functools.partial(_mixer_out_kernel, n_compute=n_compute, has_tail=tail is not None),
        grid=(ntile,),
        in_specs=in_specs,
        out_specs=[out(D_MODEL), out(D_MODEL), out(LANES)],
        out_shape=[jax.ShapeDtypeStruct((ntile * tm, D_MODEL), F32),
                   jax.ShapeDtypeStruct((ntile * tm, D_MODEL), BF16),
                   jax.ShapeDtypeStruct((ntile * tm, LANES), F32)],
        compiler_params=_cparams(("parallel",)),
        name="mixer_out",
    )(*args)


def _route_kernel(logit_ref, lt_ref, ws_ref, lr_ref, lrt_ref, base_ref, tot_ref, carry_ref, *, n_valid, ntiles):
    i = pl.program_id(0)
    tm = TOKEN_TILE

    @pl.when(i == 0)
    def _():
        carry_ref[...] = jnp.zeros_like(carry_ref)

    x = logit_ref[...]
    lane = lax.broadcasted_iota(I32, (tm, LANES), 1)
    rowg = i * tm + lax.broadcasted_iota(I32, (tm, LANES), 0)
    sels, vals = [], []
    for _ in range(TOP_K):
        m = jnp.max(x, axis=1, keepdims=True)
        idx = jnp.min(jnp.where(x == m, lane, LANES), axis=1, keepdims=True)
        sel = lane == idx
        sels.append(sel)
        vals.append(m)
        x = jnp.where(sel, -jnp.inf, x)
    ex = [jnp.exp(v - vals[0]) for v in vals]
    den = ex[0] + ex[1] + ex[2] + ex[3]
    w = jnp.zeros((tm, LANES), F32)
    msk = jnp.zeros((tm, LANES), F32)
    kth = jnp.full((tm, LANES), -1.0, F32)
    live = rowg < n_valid
    for k, (sel, e) in enumerate(zip(sels, ex)):
        pick = sel & live
        w = jnp.where(pick, e / den, w)
        msk = jnp.where(pick, 1.0, msk)
        kth = jnp.where(pick, float(k), kth)
    hi, mid, lo = _split3(w)
    ws = hi.astype(F32) + pltpu.roll(mid.astype(F32), N_EXPERTS, 1) + pltpu.roll(lo.astype(F32), 2 * N_EXPERTS, 1)
    ws_ref[...] = ws.astype(BF16)
    rank = _bdot(lt_ref[...], msk.astype(BF16))
    code = jnp.where(msk > 0.0, rank + ROUTE_K_STRIDE * kth, -1.0)
    lr_ref[...] = code.astype(I32)
    lrt_ref[...] = code.T.astype(I32)
    base_ref[...] = carry_ref[...].astype(I32)
    carry_ref[...] = carry_ref[...] + jnp.sum(msk, axis=0, keepdims=True)

    @pl.when(i == ntiles - 1)
    def _():
        tot_ref[...] = carry_ref[...].astype(I32)


def _route(logits, n_valid):
    npad = logits.shape[0]
    tm = TOKEN_TILE
    ntiles = npad // tm
    r = jnp.arange(tm)
    lt = (r[:, None] > r[None, :]).astype(BF16)
    return pl.pallas_call(
        functools.partial(_route_kernel, n_valid=n_valid, ntiles=ntiles),
        grid=(ntiles,),
        in_specs=[pl.BlockSpec((tm, LANES), lambda i: (i, 0)),
                  pl.BlockSpec((tm, tm), lambda i: (0, 0))],
        out_specs=[pl.BlockSpec((tm, LANES), lambda i: (i, 0)),
                   pl.BlockSpec((tm, LANES), lambda i: (i, 0)),
                   pl.BlockSpec((LANES, tm), lambda i: (0, i)),
                   pl.BlockSpec((None, 1, LANES), lambda i: (i, 0, 0)),
                   pl.BlockSpec((1, LANES), lambda i: (0, 0))],
        out_shape=[jax.ShapeDtypeStruct((npad, LANES), BF16),
                   jax.ShapeDtypeStruct((npad, LANES), I32),
                   jax.ShapeDtypeStruct((LANES, npad), I32),
                   jax.ShapeDtypeStruct((ntiles, 1, LANES), I32),
                   jax.ShapeDtypeStruct((1, LANES), I32)],
        scratch_shapes=[pltpu.VMEM((1, LANES), F32)],
        compiler_params=_cparams(("arbitrary",)),
        name="route",
    )(logits, lt)


def _stage_dest(code, vec, axis):
    rank = code & (ROUTE_K_STRIDE - 1)
    kth = code >> ROUTE_K_SHIFT
    out = []
    for k in range(TOP_K):
        hit = kth == k
        dest = jnp.sum(jnp.where(hit, rank + vec, 0), axis=axis, keepdims=True)
        has = jnp.sum(jnp.where(hit, 1, 0), axis=axis, keepdims=True)
        out.append(jnp.where(has > 0, dest, -1))
    return out


def _scatter_kernel(nslot_ref, dst_ref, srow_ref, a_ref, cnt_ref, h2_ref, ws_ref, codet_ref, vect_ref, xw_in_ref,
                    xw_ref, stage_ref, carry_ref, sem_ref):
    del xw_in_ref
    i = pl.program_id(0)
    tm = h2_ref.shape[0]

    @pl.when(i == 0)
    def _():
        carry_ref[...] = jnp.zeros_like(carry_ref)

    buf = i % 2
    rows = lax.broadcasted_iota(I32, (MOE_STAGE_ROWS, tm), 0)
    p = jnp.zeros((MOE_STAGE_ROWS, tm), F32)
    for dest in _stage_dest(codet_ref[...], vect_ref[...], 0):
        p = jnp.where(rows == dest, 1.0, p)
    p = p.astype(BF16)
    stage_ref[buf, :, :D_MODEL] = _bdot(p, h2_ref[...])
    stage_ref[buf, :, D_MODEL:] = _bdot(p, ws_ref[...])

    def fix(e, _):
        cnt = cnt_ref[i * N_EXPERTS + e]

        @pl.when(cnt > 0)
        def _():
            r0 = pl.multiple_of(srow_ref[i * N_EXPERTS + e], 8)
            end = a_ref[i * N_EXPERTS + e] + cnt
            stage_ref[buf, pl.ds(r0, 8), :] = stage_ref[buf, pl.ds(r0, 8), :] + carry_ref[e]
            q = pl.multiple_of(jnp.minimum(r0 + end // 8 * 8, MOE_STAGE_ROWS - 8), 8)
            carry_ref[e] = jnp.where(end % 8 > 0, stage_ref[buf, pl.ds(q, 8), :], 0.0)

        return 0

    lax.fori_loop(0, N_EXPERTS, fix, 0)

    def copy(tile, b, s):
        src = stage_ref.at[b, pl.ds(pl.multiple_of(s * MOE_CHUNK, MOE_CHUNK), MOE_CHUNK), :]
        dst = xw_ref.at[pl.ds(pl.multiple_of(dst_ref[tile * MOE_SLOTS + s], 8), MOE_CHUNK), :]
        return pltpu.make_async_copy(src, dst, sem_ref.at[b, s])

    def start(s, _):
        copy(i, buf, s).start()
        return 0

    def wait_prev(s, _):
        copy(i - 1, 1 - buf, s).wait()
        return 0

    def wait_own(s, _):
        copy(i, buf, s).wait()
        return 0

    @pl.when(i > 0)
    def _():
        lax.fori_loop(0, nslot_ref[jnp.maximum(i - 1, 0)], wait_prev, 0)

    lax.fori_loop(0, nslot_ref[i], start, 0)

    @pl.when(i == pl.num_programs(0) - 1)
    def _():
        lax.fori_loop(0, nslot_ref[i], wait_own, 0)


def _scatter(plan, h2, ws, codet, xw_zero):
    npad = h2.shape[0]
    tm = TOKEN_TILE
    tile = lambda w: pl.BlockSpec((tm, w), lambda i, *_: (i, 0))
    return pl.pallas_call(
        _scatter_kernel,
        grid_spec=pltpu.PrefetchScalarGridSpec(
            num_scalar_prefetch=5, grid=(npad // tm,),
            in_specs=[tile(D_MODEL), tile(LANES),
                      pl.BlockSpec((N_EXPERTS, tm), lambda i, *_: (0, i)),
                      pl.BlockSpec((None, N_EXPERTS, 1), lambda i, *_: (i, 0, 0)),
                      pl.BlockSpec(memory_space=pl.ANY)],
            out_specs=pl.BlockSpec(memory_space=pl.ANY),
            scratch_shapes=[pltpu.VMEM((2, MOE_STAGE_ROWS, XW_COLS), F32),
                            pltpu.VMEM((N_EXPERTS, 8, XW_COLS), F32),
                            pltpu.SemaphoreType.DMA((2, MOE_SLOTS))]),
        out_shape=jax.ShapeDtypeStruct(xw_zero.shape, F32),
        input_output_aliases={9: 0},
        compiler_params=_cparams(("arbitrary",)),
        name="moe_scatter",
    )(plan['nslot'], plan['dst'], plan['srow'], plan['a'], plan['cnt'], h2, ws, codet, plan['vect'], xw_zero)


def _expert_kernel(be_ref, nu_ref, xw_ref, w1_ref, b1_ref, w2_ref, b2_ref, y_ref, w1b_ref, w2b_ref):
    i = pl.program_id(0)
    e = be_ref[i]

    @pl.when(i < nu_ref[0])
    def _():
        @pl.when((i == 0) | (be_ref[jnp.maximum(i - 1, 0)] != e))
        def _():
            w1b_ref[...] = w1_ref[...].astype(BF16)
            w2b_ref[...] = w2_ref[...].astype(BF16)

        xw = xw_ref[...]
        wl = xw[:, D_MODEL:]
        lane = lax.broadcasted_iota(I32, wl.shape, 1)
        pick = ((lane & (N_EXPERTS - 1)) == e) & (lane < 3 * N_EXPERTS)
        wrow = jnp.sum(jnp.where(pick, wl, 0.0), axis=1, keepdims=True)
        u = _bdot(xw[:, :D_MODEL].astype(BF16), w1b_ref[...]) + b1_ref[...]
        g = jnp.minimum(u[:, :D_FF], SWIGLU_LIMIT)
        lin = jnp.clip(u[:, D_FF:], -SWIGLU_LIMIT, SWIGLU_LIMIT)
        act = g * jax.nn.sigmoid(SWIGLU_ALPHA * g) * (lin + 1.0)
        y = _bdot(act.astype(BF16), w2b_ref[...]) + b2_ref[...]
        y_ref[...] = y * wrow

    @pl.when(i >= nu_ref[0])
    def _():
        y_ref[...] = jnp.zeros_like(y_ref)


def _experts(layer, block_e, n_used, xw, w1, b1, w2, b2):
    rmax = xw.shape[0]
    nblk = rmax // MOE_BLOCK

    def row_map(i, be, nu):
        return (jnp.minimum(i, nu[0] - 1), 0)

    def expert_map(i, be, nu):
        return (layer, be[i], 0, 0)

    return pl.pallas_call(
        _expert_kernel,
        grid_spec=pltpu.PrefetchScalarGridSpec(
            num_scalar_prefetch=2, grid=(nblk,),
            in_specs=[pl.BlockSpec((MOE_BLOCK, XW_COLS), row_map),
                      pl.BlockSpec((None, None, D_MODEL, 2 * D_FF), expert_map),
                      pl.BlockSpec((None, None, 1, 2 * D_FF), expert_map),
                      pl.BlockSpec((None, None, D_FF, D_MODEL), expert_map),
                      pl.BlockSpec((None, None, 1, D_MODEL), expert_map)],
            out_specs=pl.BlockSpec((MOE_BLOCK, D_MODEL), lambda i, be, nu: (i, 0)),
            scratch_shapes=[pltpu.VMEM((D_MODEL, 2 * D_FF), BF16), pltpu.VMEM((D_FF, D_MODEL), BF16)]),
        out_shape=jax.ShapeDtypeStruct((rmax, D_MODEL), F32),
        compiler_params=_cparams(("arbitrary",)),
        name="moe_experts",
    )(block_e, n_used, xw, w1, b1, w2, b2)


def _combine_kernel(nslot_ref, dst_ref, code_ref, vec_ref, xmid_ref, gfp_ref, gfs_ref, y_ref, o_ref, ybuf_ref,
                    sem_ref):
    i = pl.program_id(0)
    tm = code_ref.shape[0]
    buf = i % 2

    def copy(tile, b, s):
        src = y_ref.at[pl.ds(pl.multiple_of(dst_ref[tile * MOE_SLOTS + s], 8), MOE_CHUNK), :]
        dst = ybuf_ref.at[b, pl.ds(pl.multiple_of(s * MOE_CHUNK, MOE_CHUNK), MOE_CHUNK), :]
        return pltpu.make_async_copy(src, dst, sem_ref.at[b, s])

    def fetch(tile, b):
        def start(s, _):
            copy(tile, b, s).start()
            return 0

        def clear(s, _):
            ybuf_ref[b, pl.ds(pl.multiple_of(s * MOE_CHUNK, MOE_CHUNK), MOE_CHUNK), :] = jnp.zeros(
                (MOE_CHUNK, D_MODEL), F32)
            return 0

        lax.fori_loop(0, nslot_ref[tile], start, 0)
        lax.fori_loop(nslot_ref[tile], MOE_SLOTS, clear, 0)

    def wait(s, _):
        copy(i, buf, s).wait()
        return 0

    @pl.when(i == 0)
    def _():
        fetch(i, buf)

    @pl.when(i + 1 < pl.num_programs(0))
    def _():
        fetch(jnp.minimum(i + 1, pl.num_programs(0) - 1), 1 - buf)

    cols = lax.broadcasted_iota(I32, (tm, MOE_STAGE_ROWS), 1)
    pt = jnp.zeros((tm, MOE_STAGE_ROWS), F32)
    for dest in _stage_dest(code_ref[...], vec_ref[...], 1):
        pt = jnp.where(cols == dest, 1.0, pt)
    lax.fori_loop(0, nslot_ref[i], wait, 0)
    acc = _bdot(pt.astype(BF16), ybuf_ref[buf].astype(BF16))
    gate = jnp.where(i == pl.num_programs(0) - 1, gfs_ref[...], gfp_ref[...])
    o_ref[...] = xmid_ref[...] + gate * acc


def _combine(plan, code, xmid, gf_prompt, gf_sample, tiles_per_seq, ypad):
    npad = xmid.shape[0]
    tm = TOKEN_TILE
    last_seq = gf_prompt.shape[0] - 1
    tile = lambda w: pl.BlockSpec((tm, w), lambda i, *_: (i, 0))
    return pl.pallas_call(
        _combine_kernel,
        grid_spec=pltpu.PrefetchScalarGridSpec(
            num_scalar_prefetch=2, grid=(npad // tm,),
            in_specs=[tile(LANES),
                      pl.BlockSpec((None, 1, LANES), lambda i, *_: (i, 0, 0)),
                      tile(D_MODEL),
                      pl.BlockSpec((None, 1, D_MODEL), lambda i, *_: (jnp.minimum(i // tiles_per_seq, last_seq), 0, 0)),
                      pl.BlockSpec((tm, D_MODEL), lambda i, *_: (0, 0)),
                      pl.BlockSpec(memory_space=pl.ANY)],
            out_specs=tile(D_MODEL),
            scratch_shapes=[pltpu.VMEM((2, MOE_STAGE_ROWS, D_MODEL), F32),
                            pltpu.SemaphoreType.DMA((2, MOE_SLOTS))]),
        out_shape=jax.ShapeDtypeStruct((npad, D_MODEL), F32),
        compiler_params=_cparams(("arbitrary",)),
        name="moe_combine",
    )(plan['nslot'], plan['dst'], code, plan['vec'], xmid, gf_prompt, gf_sample, ypad)


def _moe_plan(base, total):
    ntiles = base.shape[0]
    base = base[:, 0, :N_EXPERTS]
    counts = total[0, :N_EXPERTS]
    cap = (counts + MOE_CHUNK + MOE_BLOCK - 1) // MOE_BLOCK * MOE_BLOCK
    pend = jnp.cumsum(cap)
    pstart = pend - cap
    cnt = jnp.concatenate([base[1:], counts[None]], axis=0) - base
    off = pstart[None, :] + base
    a = off % 8
    nch = jnp.where(cnt > 0, (a + cnt + MOE_CHUNK - 1) // MOE_CHUNK, 0)
    cum = jnp.cumsum(nch, axis=1)
    slot0 = cum - nch
    srow = slot0 * MOE_CHUNK
    vec = srow + a
    s = jnp.arange(MOE_SLOTS)
    e_of_s = jnp.minimum(jnp.sum(s[None, :, None] >= cum[:, None, :], axis=2), N_EXPERTS - 1)
    pick = e_of_s[:, :, None] == jnp.arange(N_EXPERTS)[None, None, :]
    c_of_s = s[None, :] - jnp.sum(jnp.where(pick, slot0[:, None, :], 0), axis=2)
    dst = jnp.sum(jnp.where(pick, (off - a)[:, None, :], 0), axis=2) + c_of_s * MOE_CHUNK
    dst = jnp.where(s[None, :] < cum[:, -1:], dst, 0)
    rmax = (ntiles * TOKEN_TILE * TOP_K + N_EXPERTS * (MOE_CHUNK + MOE_BLOCK)) // MOE_BLOCK * MOE_BLOCK
    n_used = pend[-1] // MOE_BLOCK
    blk = jnp.arange(rmax // MOE_BLOCK)
    row0 = jnp.minimum(blk, n_used - 1) * MOE_BLOCK
    block_e = jnp.minimum(jnp.sum(row0[:, None] >= pend[None, :], axis=1), N_EXPERTS - 1)
    flat = lambda v: v.reshape(-1).astype(I32)
    pad_lanes = jnp.zeros((ntiles, LANES - N_EXPERTS), I32)
    return dict(nslot=flat(cum[:, -1]), dst=flat(dst), srow=flat(srow), a=flat(a), cnt=flat(cnt),
                vect=vec.astype(I32)[:, :, None],
                vec=jnp.concatenate([vec.astype(I32), pad_lanes], axis=1)[:, None, :],
                block_e=block_e.astype(I32), n_used=n_used.reshape(1).astype(I32), rmax=rmax)


def _final_kernel(x_ref, g_ref, o_ref):
    o_ref[...] = _rms(x_ref[...], g_ref[...])


def _final_norm(xall, g):
    npad = xall.shape[0]
    tm = TOKEN_TILE
    return pl.pallas_call(
        _final_kernel,
        grid=(npad // tm,),
        in_specs=[pl.BlockSpec((tm, D_MODEL), lambda i: (i, 0)), pl.BlockSpec((1, D_MODEL), lambda i: (0, 0))],
        out_specs=pl.BlockSpec((tm, D_MODEL), lambda i: (i, 0)),
        out_shape=jax.ShapeDtypeStruct((npad, D_MODEL), F32),
        compiler_params=_cparams(("parallel",)),
        name="final_norm",
    )(xall, g)


def _rot_cols(w):
    half = w.shape[-1] // 2
    return jnp.concatenate([-w[..., half:], w[..., :half]], axis=-1)


def _layer_weights(l, w_in, mla_w_uq, mla_w_uk, mla_w_uv, w_branch, w_out, router_w, router_b):
    idx = [0]
    for s in W_IN_SIZES:
        idx.append(idx[-1] + s)
    w = w_in[l]
    kr = w[:, idx[2]:idx[3]]
    pad = jnp.zeros((D_MODEL, W_EXT_HEAD - KR_ROT_COL - MLA_ROPE), F32)
    seg0 = jnp.concatenate([w[:, :idx[3]], _rot_cols(kr), pad], axis=1)
    w_ext = jnp.concatenate([seg0, w[:, idx[3]:]], axis=1).astype(BF16)
    uq = mla_w_uq[l].reshape(MLA_Q_LORA, MLA_HEADS, MLA_NOPE + MLA_ROPE)
    nope = uq[:, :, :MLA_NOPE].reshape(MLA_Q_LORA, -1)
    pe = uq[:, :, MLA_NOPE:]
    wuq = jnp.concatenate([nope, pe.reshape(MLA_Q_LORA, -1), _rot_cols(pe).reshape(MLA_Q_LORA, -1)], axis=1).astype(BF16)
    wuk = jnp.transpose(mla_w_uk[l], (1, 2, 0)).astype(BF16)
    wuv = jnp.transpose(mla_w_uv[l], (1, 0, 2))
    z = jnp.zeros_like(wuv[0])
    wuv_bd = jnp.stack([jnp.concatenate([jnp.concatenate([wuv[2 * p], z], axis=1),
                                         jnp.concatenate([z, wuv[2 * p + 1]], axis=1)], axis=0)
                        for p in range(MLA_HEADS // 2)]).astype(BF16)
    wuv_all = mla_w_uv[l].reshape(MLA_KV_LORA, BRANCH_W).astype(BF16)
    rw = jnp.concatenate([router_w[l], jnp.zeros((D_MODEL, LANES - N_EXPERTS), F32)], axis=1).astype(BF16)
    rb = jnp.concatenate([router_b[l], jnp.full((LANES - N_EXPERTS,), NEG, F32)])[None]
    return dict(w_ext=w_ext, wuq=wuq, wuk=wuk, wuv_bd=wuv_bd, wuv_all=wuv_all,
                wbr=w_branch[l].astype(BF16), wout=w_out[l].astype(BF16), rw=rw, rb=rb)


def _rope_tables(pos):
    half = MLA_ROPE // 2
    inv = ROPE_THETA ** (-jnp.arange(half, dtype=F32) / half)
    ang = pos.astype(F32)[:, None] * inv[None, :]
    cos = jnp.tile(jnp.cos(ang), (1, 2 * MLA_HEADS))
    sin = jnp.tile(jnp.sin(ang), (1, 2 * MLA_HEADS))
    return cos, sin


def _pad_rows(a, rows):
    return jnp.concatenate([a, jnp.zeros((rows - a.shape[0],) + a.shape[1:], a.dtype)], axis=0)


def kernel(x_prompt, x_sample, c_prompt, c_sample, cache_mla_ckv, cache_mla_krope, cache_sb_k, cache_sb_v, state_hgrn, page_table, ada_w, ada_b, norm_mix, norm_ffn, w_in, mla_q_norm, mla_kv_norm, mla_w_uq, mla_w_uk, mla_w_uv, hg_lower_bounds, hg_norm, w_branch, w_out, router_w, router_b, exp_w1, exp_b1, exp_w2, exp_b2, final_norm):
    bp, tp, d = x_prompt.shape
    bs, ts, _ = x_sample.shape
    depth = w_in.shape[0]
    tm = TOKEN_TILE
    n_p, n_s = bp * tp, bs * ts
    assert d == D_MODEL and tp % tm == 0 and n_s <= tm and tp % MLA_TK == 0
    npg = page_table.shape[1]
    assert npg % MLA_PAGES == 0 and npg % SB_PAGES == 0 and cache_mla_ckv.shape[2] == PAGE_SIZE
    past_len = npg * PAGE_SIZE
    nt_p = tp // tm
    tile_s = n_p // tm
    npad = n_p + tm
    ts_pad = HG_CHUNK

    xall = jnp.concatenate([x_prompt.reshape(n_p, d), _pad_rows(x_sample.reshape(n_s, d), tm)], axis=0)
    mod = _adaln(jnp.concatenate([c_prompt, c_sample], axis=0), ada_w, ada_b)
    lower = _lower_bound(hg_lower_bounds)
    cos_p, sin_p = _rope_tables(jnp.arange(tp))
    cos_s, sin_s = _rope_tables(past_len + jnp.arange(ts))
    cos_s = _pad_rows(jnp.tile(cos_s, (bs, 1)), tm)
    sin_s = _pad_rows(jnp.tile(sin_s, (bs, 1)), tm)

    r = jnp.arange(SB_T)
    u_sb = (r[:, None] >= r[None, :]).astype(BF16)
    u_page = u_sb[:PAGE_SIZE, :PAGE_SIZE]
    head_mask = (jnp.arange(BRANCH_W)[None, :] // SB_DIM == jnp.arange(SB_HEADS)[:, None]).astype(F32)
    cache_k = jnp.transpose(cache_sb_k, (0, 1, 3, 4, 2)).reshape(cache_sb_k.shape[:2] + (BRANCH_W, PAGE_SIZE))
    cache_v = jnp.transpose(cache_sb_v, (0, 1, 3, 4, 2)).reshape(cache_sb_v.shape[:2] + (BRANCH_W, PAGE_SIZE))
    cache_krt = jnp.transpose(cache_mla_krope, (0, 1, 3, 2))
    zero_state = jnp.zeros((bp, HG_HEADS, HG_DK, HG_DV), F32)

    def prompt_mod(v):
        return v[:bp, None, :]

    def sample_mod(v):
        return _pad_rows(jnp.repeat(v[bp:], ts, axis=0), tm)[None]

    outs = [[] for _ in range(10)]
    for l in range(depth):
        lw = _layer_weights(l, w_in, mla_w_uq, mla_w_uk, mla_w_uv, w_branch, w_out, router_w, router_b)
        sh_m, sc_m, g_m, sh_f, sc_f, g_f = jnp.split(mod[l], 6, axis=-1)
        gn = norm_mix[l][None]
        qn = mla_q_norm[l][None]
        kvn = mla_kv_norm[l][None]
        lb = lower[l][None]

        (qcat, kcat, ckv, kr, sq, skb, svb, sk, sv, hq, hk, hv, hg, hgate, sig) = _mixer_in(
            xall, 0, bp, nt_p, gn, prompt_mod(sh_m), prompt_mod(sc_m), lw['w_ext'], qn, kvn,
            lw['wuq'], lw['wuk'], cos_p, sin_p, lb)
        o_mla = _mla_prompt(qcat, kcat, lw['wuv_bd'], bp, tp)
        o_sb = _sb_prompt(sq, skb, svb, u_sb, bp, tp)
        o_hg, st_p = _hgrn(hq, hk, hv, hg, zero_state, bp, tp)
        for lst, val in zip(outs[:5], (ckv.reshape(bp, tp, -1), kr.reshape(bp, tp, -1),
                                       sk.reshape(bp, tp, SB_HEADS, SB_DIM), sv.reshape(bp, tp, SB_HEADS, SB_DIM),
                                       st_p)):
            lst.append(val)

        (qcat_s, kcat_s, ckv_s, kr_s, sq_s, skb_s, svb_s, sk_s, sv_s, hq_s, hk_s, hv_s, hg_s, hgate_s, sig_s) = _mixer_in(
            xall, tile_s, 1, 1, gn, sample_mod(sh_m), sample_mod(sc_m), lw['w_ext'], qn, kvn,
            lw['wuq'], lw['wuk'], cos_s, sin_s, lb)
        q_s = jnp.transpose(qcat_s[:, :n_s].reshape(MLA_HEADS, bs, ts, MLA_CAT), (1, 2, 0, 3)).reshape(bs, ts * MLA_HEADS, MLA_CAT)
        kn_s = jnp.concatenate([kcat_s[:n_s].reshape(bs, ts, MLA_CAT),
                                jnp.zeros((bs, 16 - ts, MLA_CAT), BF16)], axis=1)
        o_mla_s = _mla_sample(l, page_table, q_s, kn_s, cache_mla_ckv, cache_krt, lw['wuv_all'], head_mask)
        q_bd = (sq_s[:n_s].reshape(bs, ts, 1, BRANCH_W) * head_mask.astype(BF16)[None, None]).reshape(bs, ts * SB_HEADS, BRANCH_W)
        pad_new = lambda a: jnp.concatenate([a[:n_s].reshape(bs, ts, BRANCH_W),
                                             jnp.zeros((bs, PAGE_SIZE - ts, BRANCH_W), BF16)], axis=1)
        o_sb_s = _sb_sample(l, page_table, q_bd, pad_new(skb_s), pad_new(svb_s), cache_k, cache_v, u_page, head_mask)
        pad_t = lambda a: jnp.concatenate([a[:n_s].reshape(bs, ts, 512),
                                           jnp.zeros((bs, ts_pad - ts, 512), F32)], axis=1).reshape(bs * ts_pad, 512)
        o_hg_s, st_s = _hgrn(pad_t(hq_s), pad_t(hk_s), pad_t(hv_s), pad_t(hg_s), state_hgrn[l], bs, ts_pad)
        o_hg_s = o_hg_s.reshape(bs, ts_pad, 512)[:, :ts].reshape(n_s, 512)
        out_weights = (hg_norm[l][None], lw['wbr'], lw['wout'], norm_ffn[l][None], lw['rw'], lw['rb'])
        tail = _mixer_out(
            xall, tile_s, 1, 1, _pad_rows(o_mla_s.reshape(n_s, 512), tm), _pad_rows(o_sb_s.reshape(n_s, 512), tm),
            _pad_rows(o_hg_s, tm), hgate_s, sig_s, sample_mod(g_m), sample_mod(sh_f), sample_mod(sc_f),
            *out_weights, None)
        xmid, h2, logits = _mixer_out(xall, 0, bp, nt_p, o_mla, o_sb, o_hg, hgate, sig,
                                      prompt_mod(g_m), prompt_mod(sh_f), prompt_mod(sc_f), *out_weights, tail)
        for lst, val in zip(outs[5:], (ckv_s[:n_s].reshape(bs, ts, -1), kr_s[:n_s].reshape(bs, ts, -1),
                                       sk_s[:n_s].reshape(bs, ts, SB_HEADS, SB_DIM),
                                       sv_s[:n_s].reshape(bs, ts, SB_HEADS, SB_DIM), st_s)):
            lst.append(val)

        ws, code, codet, base, total = _route(logits, n_p + n_s)
        plan = _moe_plan(base, total)
        xw = _scatter(plan, h2, ws, codet, jnp.zeros((plan['rmax'], XW_COLS), F32))
        ypad = _experts(l, plan['block_e'], plan['n_used'], xw, exp_w1, exp_b1[:, :, None, :],
                        exp_w2, exp_b2[:, :, None, :])
        xall = _combine(plan, code, xmid, prompt_mod(g_f), sample_mod(g_f)[0], nt_p, ypad)

    y = _final_norm(xall, final_norm[None])
    y_prompt = y[:n_p].reshape(bp, tp, d)
    y_sample = y[n_p:n_p + n_s].reshape(bs, ts, d)
    st = [jnp.stack(o) for o in outs]
    return (y_prompt, y_sample, st[0], st[1], st[2], st[3], st[4], st[5], st[6], st[7], st[8], st[9])
```
